```python
import math
import jax
import jax.numpy as jnp
from jax import lax
import numpy as np

D_MODEL = 1024
BATCH = 16
SEQ = 2048
DEPTH = 1

CHUNK = 64
N_META = 16
QBLK = 128

MLA_HEADS = 4
D_NOPE = 128
D_ROPE = 64
D_QK = D_NOPE + D_ROPE
D_V = 128
KV_RANK = 256
Q_RANK = 384
ROPE_THETA = 10000.0
MLA_WIDTH = MLA_HEADS * D_V

LRU_WIDTH = D_MODEL // 2
LRU_BLOCKS = 8
LRU_BLOCK = LRU_WIDTH // LRU_BLOCKS
CONV_W = 4
C_RGLRU = 8.0

MIX_WIDTH = MLA_WIDTH + LRU_WIDTH
IN_WIDTH = Q_RANK + KV_RANK + D_ROPE + LRU_WIDTH + LRU_WIDTH

D_FF = 2816
FFN_RESIDUAL = 0.5
EPS = 1e-6
NEG_INF = -1e30

kernel_name = "hymba_mla_rglru_macaron_block"


def _rmsnorm(x, g):
    xf = x.astype(jnp.float32)
    y = xf * lax.rsqrt(jnp.mean(xf * xf, axis=-1, keepdims=True) + EPS)
    return (y * g.astype(jnp.float32)).astype(x.dtype)


def _swiglu_half(h, g, w_gate, w_up, w_down):
    u = _rmsnorm(h, g)
    return FFN_RESIDUAL * ((jax.nn.silu(u @ w_gate) * (u @ w_up)) @ w_down)


def _rope(x, cos, sin):
    half = x.shape[-1] // 2
    x1, x2 = x[..., :half], x[..., half:]
    return jnp.concatenate([x1 * cos - x2 * sin, x2 * cos + x1 * sin], axis=-1)


def _mla(c_q, c_kv, k_r, q_latent_norm, w_uq, kv_latent_norm, w_uk, w_uv,
         q_head_norm, k_head_norm):
    B, L, _ = c_q.shape
    q = (_rmsnorm(c_q, q_latent_norm) @ w_uq).reshape(B, L, MLA_HEADS, D_QK)
    ckv = _rmsnorm(c_kv, kv_latent_norm)
    k_nope = (ckv @ w_uk).reshape(B, L, MLA_HEADS, D_NOPE)
    v = (ckv @ w_uv).reshape(B, L, MLA_HEADS, D_V)
    k_rope = jnp.broadcast_to(k_r[:, :, None, :], (B, L, MLA_HEADS, D_ROPE))
    k = jnp.concatenate([k_nope, k_rope], axis=-1)
    q = _rmsnorm(q, q_head_norm)
    k = _rmsnorm(k, k_head_norm)
    pos = jnp.arange(L, dtype=jnp.float32)
    inv_freq = ROPE_THETA ** (-jnp.arange(0, D_ROPE // 2, dtype=jnp.float32) / (D_ROPE // 2))
    ang = pos[:, None] * inv_freq[None, :]
    cos = jnp.cos(ang)[:, None, :].astype(q.dtype)
    sin = jnp.sin(ang)[:, None, :].astype(q.dtype)
    q = jnp.concatenate([q[..., :D_NOPE], _rope(q[..., D_NOPE:], cos, sin)], axis=-1)
    k = jnp.concatenate([k[..., :D_NOPE], _rope(k[..., D_NOPE:], cos, sin)], axis=-1)
    n_blk = -(-L // QBLK)
    L_pad = n_blk * QBLK
    padw = ((0, 0), (0, L_pad - L), (0, 0), (0, 0))
    q = jnp.pad(q, padw).transpose(0, 2, 1, 3)
    k = jnp.pad(k, padw).transpose(0, 2, 1, 3)
    v = jnp.pad(v, padw).transpose(0, 2, 1, 3)
    cid = (jnp.arange(L_pad, dtype=jnp.int32) + (CHUNK - N_META)) // CHUNK
    q_blocks = q.reshape(B, MLA_HEADS, n_blk, QBLK, D_QK).transpose(2, 0, 1, 3, 4)
    cid_blocks = cid.reshape(n_blk, QBLK)
    scale = 1.0 / math.sqrt(D_QK)

    def attend(args):
        qb, cq = args
        s = jnp.einsum('bhqd,bhkd->bhqk', qb, k,
                       preferred_element_type=jnp.float32) * scale
        mask = cid[None, :] <= cq[:, None]
        s = jnp.where(mask[None, None], s, NEG_INF)
        p = jax.nn.softmax(s, axis=-1).astype(v.dtype)
        return jnp.einsum('bhqk,bhkd->bhqd', p, v)

    o = lax.map(attend, (q_blocks, cid_blocks))
    o = o.transpose(1, 0, 3, 2, 4).reshape(B, L_pad, MLA_WIDTH)
    return o[:, :L]


def _rglru(u, gate, conv_w, conv_b, gate_a_w, gate_a_b, gate_x_w, gate_x_b, lru_lambda):
    B, L, W = u.shape
    xc = lax.conv_general_dilated(
        u, conv_w[:, None, :].astype(u.dtype), window_strides=(1,),
        padding=[(CONV_W - 1, 0)], dimension_numbers=('NWC', 'WIO', 'NWC'),
        feature_group_count=W) + conv_b
    xb = xc.reshape(B, L, LRU_BLOCKS, LRU_BLOCK)
    r = jax.nn.sigmoid(jnp.einsum('blni,nij->blnj', xb, gate_a_w).reshape(B, L, W) + gate_a_b)
    i = jax.nn.sigmoid(jnp.einsum('blni,nij->blnj', xb, gate_x_w).reshape(B, L, W) + gate_x_b)
    log_a = -C_RGLRU * r.astype(jnp.float32) * jax.nn.softplus(-lru_lambda.astype(jnp.float32))
    a = jnp.exp(log_a)
    mult = jnp.sqrt(-jnp.expm1(2.0 * log_a))
    first = (jnp.arange(L) == 0)[None, :, None]
    mult = jnp.where(first, 1.0, mult)
    b = mult * (i * xc).astype(jnp.float32)

    def combine(lhs, rhs):
        a1, b1 = lhs
        a2, b2 = rhs
        return a1 * a2, a2 * b1 + b2

    _, h = lax.associative_scan(combine, (a, b), axis=1)
    return h.astype(u.dtype) * jax.nn.gelu(gate)


def setup_inputs(seed: int = 0) -> dict:
    key = jax.random.key(seed)
    ks = iter(jax.random.split(key, 40))

    def dense(shape, fan_in):
        return jax.random.normal(next(ks), shape, jnp.float32) * (fan_in ** -0.5)

    def gain(n):
        return 1.0 + 0.05 * jax.random.normal(next(ks), (DEPTH, n), jnp.float32)

    def bias(n):
        return 0.01 * jax.random.normal(next(ks), (DEPTH, n), jnp.float32)

    x = jax.random.normal(next(ks), (BATCH, SEQ, D_MODEL), jnp.float32)
    meta_tokens = jax.random.normal(next(ks), (N_META, D_MODEL), jnp.float32)
    a0 = 0.9 + 0.099 * jax.random.uniform(next(ks), (DEPTH, LRU_WIDTH), jnp.float32)
    s0 = a0 ** (1.0 / C_RGLRU)
    lru_lambda = jnp.log(s0) - jnp.log1p(-s0)
    return {
        "x": x,
        "meta_tokens": meta_tokens,
        "ffn1_norm": gain(D_MODEL),
        "ffn1_w_gate": dense((DEPTH, D_MODEL, D_FF), D_MODEL),
        "ffn1_w_up": dense((DEPTH, D_MODEL, D_FF), D_MODEL),
        "ffn1_w_down": dense((DEPTH, D_FF, D_MODEL), D_FF),
        "mix_norm": gain(D_MODEL),
        "w_in": dense((DEPTH, D_MODEL, IN_WIDTH), D_MODEL),
        "q_latent_norm": gain(Q_RANK),
        "w_uq": dense((DEPTH, Q_RANK, MLA_HEADS * D_QK), Q_RANK),
        "kv_latent_norm": gain(KV_RANK),
        "w_uk": dense((DEPTH, KV_RANK, MLA_HEADS * D_NOPE), KV_RANK),
        "w_uv": dense((DEPTH, KV_RANK, MLA_HEADS * D_V), KV_RANK),
        "q_head_norm": gain(D_QK),
        "k_head_norm": gain(D_QK),
        "conv_w": dense((DEPTH, CONV_W, LRU_WIDTH), CONV_W),
        "conv_b": bias(LRU_WIDTH),
        "gate_a_w": dense((DEPTH, LRU_BLOCKS, LRU_BLOCK, LRU_BLOCK), LRU_BLOCK),
        "gate_a_b": bias(LRU_WIDTH),
        "gate_x_w": dense((DEPTH, LRU_BLOCKS, LRU_BLOCK, LRU_BLOCK), LRU_BLOCK),
        "gate_x_b": bias(LRU_WIDTH),
        "lru_lambda": lru_lambda,
        "attn_out_norm": gain(MLA_WIDTH),
        "lru_out_norm": gain(LRU_WIDTH),
        "w_out": dense((DEPTH, MIX_WIDTH, D_MODEL), MIX_WIDTH),
        "ffn2_norm": gain(D_MODEL),
        "ffn2_w_gate": dense((DEPTH, D_MODEL, D_FF), D_MODEL),
        "ffn2_w_up": dense((DEPTH, D_MODEL, D_FF), D_MODEL),
        "ffn2_w_down": dense((DEPTH, D_FF, D_MODEL), D_FF),
        "final_norm": gain(D_MODEL),
    }


def reference(x, meta_tokens, ffn1_norm, ffn1_w_gate, ffn1_w_up, ffn1_w_down,
              mix_norm, w_in, q_latent_norm, w_uq, kv_latent_norm, w_uk, w_uv,
              q_head_norm, k_head_norm, conv_w, conv_b, gate_a_w, gate_a_b,
              gate_x_w, gate_x_b, lru_lambda, attn_out_norm, lru_out_norm, w_out,
              ffn2_norm, ffn2_w_gate, ffn2_w_up, ffn2_w_down, final_norm):
    B = x.shape[0]
    meta = jnp.broadcast_to(meta_tokens.astype(x.dtype)[None], (B, N_META, D_MODEL))
    h = jnp.concatenate([meta, x], axis=1)
    o1 = Q_RANK
    o2 = o1 + KV_RANK
    o3 = o2 + D_ROPE
    o4 = o3 + LRU_WIDTH
    for l in range(DEPTH):
        h = h + _swiglu_half(h, ffn1_norm[l], ffn1_w_gate[l], ffn1_w_up[l], ffn1_w_down[l])
        z = _rmsnorm(h, mix_norm[l]) @ w_in[l]
        c_q, c_kv, k_r = z[..., :o1], z[..., o1:o2], z[..., o2:o3]
        u, g = z[..., o3:o4], z[..., o4:]
        y_mla = _mla(c_q, c_kv, k_r, q_latent_norm[l], w_uq[l], kv_latent_norm[l],
                     w_uk[l], w_uv[l], q_head_norm[l], k_head_norm[l])
        y_lru = _rglru(u, g, conv_w[l], conv_b[l], gate_a_w[l], gate_a_b[l],
                       gate_x_w[l], gate_x_b[l], lru_lambda[l])
        y = jnp.concatenate([_rmsnorm(y_mla, attn_out_norm[l]),
                             _rmsnorm(y_lru, lru_out_norm[l])], axis=-1)
        h = h + y @ w_out[l]
        h = h + _swiglu_half(h, ffn2_norm[l], ffn2_w_gate[l], ffn2_w_up[l], ffn2_w_down[l])
        h = _rmsnorm(h, final_norm[l])
    return h[:, N_META:]
```

```python
import functools
import math

import jax
import jax.numpy as jnp
from jax import lax
from jax.experimental import pallas as pl
from jax.experimental.pallas import tpu as pltpu

D_MODEL = 1024
SEQ = 2048
N_META = 16
CHUNK = 64
HEADS = 4
D_NOPE = 128
D_ROPE = 64
D_QK = D_NOPE + D_ROPE
D_V = 128
KV_RANK = 256
Q_RANK = 384
ROPE_THETA = 10000.0
LRU_WIDTH = 512
LRU_BLOCKS = 8
LRU_BLOCK = 64
CONV_W = 4
C_RGLRU = 8.0
D_FF = 2816
EPS = 1e-6
NEG_INF = -1e30

LANES = 128
D_HEAD_PAD = 2 * LANES
Z_WIDTH = Q_RANK + KV_RANK + LANES + 2 * LRU_WIDTH
GATE_GROUP = 256

TM = 256
FF_CHUNK = 256
TQ = 256
TK = 256
TS = 256
VMEM_LIMIT_BIG = 56 * 1024 * 1024
VMEM_LIMIT_SMALL = 32 * 1024 * 1024

F32 = jnp.float32
BF16 = jnp.bfloat16


def _rms(x, g):
    ms = jnp.mean(x * x, axis=-1, keepdims=True)
    return x * lax.rsqrt(ms + EPS) * g


def _dot(a, b):
    return jnp.dot(a, b, preferred_element_type=F32)


def _dot_nt(a, b):
    return lax.dot_general(a, b, (((1,), (1,)), ((), ())), preferred_element_type=F32)


def _ffn_half(h, g_ref, wg_ref, wu_ref, wd_ref, a_ref):
    u = _rms(h, g_ref[...]).astype(BF16)
    for c in range(D_FF // FF_CHUNK):
        sl = slice(c * FF_CHUNK, (c + 1) * FF_CHUNK)
        gate = _dot(u, wg_ref[:, sl])
        up = _dot(u, wu_ref[:, sl])
        a_ref[:, sl] = (jax.nn.silu(gate) * up).astype(BF16)
    return h + 0.5 * _dot(a_ref[...], wd_ref[...])


def _front_kernel(x_ref, rope_ref, n1_ref, wg_ref, wu_ref, wd_ref, nmix_ref, win_ref,
                  nq_ref, wuq_ref, nkv_ref, wuk_ref, wuv_ref, qhn_ref, khn_ref,
                  h_ref, q_ref, k_ref, v_ref, u_ref, g_ref, a_scr):
    h = _ffn_half(x_ref[0], n1_ref, wg_ref, wu_ref, wd_ref, a_scr)
    h_ref[0] = h
    z = _dot(_rms(h, nmix_ref[...]).astype(BF16), win_ref[...])
    o1 = Q_RANK
    o2 = o1 + KV_RANK
    o3 = o2 + LANES
    o4 = o3 + LRU_WIDTH
    c_q, c_kv, k_r = z[:, :o1], z[:, o1:o2], z[:, o2:o3]
    u_ref[0] = z[:, o3:o4].astype(u_ref.dtype)
    g_ref[0] = z[:, o4:].astype(g_ref.dtype)

    cos, sin_lo, sin_hi = rope_ref[0], rope_ref[1], rope_ref[2]

    def rope(x):
        return x * cos + pltpu.roll(x, LANES - D_ROPE // 2, 1) * sin_lo + pltpu.roll(x, D_ROPE // 2, 1) * sin_hi

    q_all = _dot(_rms(c_q, nq_ref[...]).astype(BF16), wuq_ref[...])
    ckv = _rms(c_kv, nkv_ref[...]).astype(BF16)
    k_nope = _dot(ckv, wuk_ref[...])
    v_ref[0] = _dot(ckv, wuv_ref[...]).astype(v_ref.dtype)

    qhn = qhn_ref[...]
    khn = khn_ref[...]
    kr_roped = rope(k_r * khn[:, LANES:])
    kr_ss = jnp.sum(k_r * k_r, axis=-1, keepdims=True)
    scale = 1.0 / math.sqrt(D_QK)
    for hd in range(HEADS):
        qh = q_all[:, hd * D_HEAD_PAD:(hd + 1) * D_HEAD_PAD]
        q_rinv = lax.rsqrt(jnp.sum(qh * qh, axis=-1, keepdims=True) / D_QK + EPS)
        qn = qh * q_rinv * qhn
        q_out = jnp.concatenate([qn[:, :LANES], rope(qn[:, LANES:])], axis=1) * scale
        q_ref[0, hd] = q_out.astype(q_ref.dtype)
        kn = k_nope[:, hd * D_NOPE:(hd + 1) * D_NOPE]
        k_rinv = lax.rsqrt((jnp.sum(kn * kn, axis=-1, keepdims=True) + kr_ss) / D_QK + EPS)
        k_out = jnp.concatenate([kn * k_rinv * khn[:, :LANES], kr_roped * k_rinv], axis=1)
        k_ref[0, hd] = k_out.astype(k_ref.dtype)


def _resident(shape):
    nd = len(shape)
    return pl.BlockSpec(shape, lambda *_: (0,) * nd, pipeline_mode=pl.Buffered(1))


def _front(x, rope_tab, w, tm):
    bsz, s, _ = x.shape
    ns = s // tm
    grid = (bsz, ns)
    tok = lambda b, i: (b, i, 0)
    in_specs = [
        pl.BlockSpec((1, tm, D_MODEL), tok),
        pl.BlockSpec((3, tm, LANES), lambda b, i: (0, i, 0)),
        _resident((1, D_MODEL)),
        _resident((D_MODEL, D_FF)), _resident((D_MODEL, D_FF)), _resident((D_FF, D_MODEL)),
        _resident((1, D_MODEL)), _resident((D_MODEL, Z_WIDTH)),
        _resident((1, Q_RANK)), _resident((Q_RANK, HEADS * D_HEAD_PAD)),
        _resident((1, KV_RANK)), _resident((KV_RANK, HEADS * D_NOPE)), _resident((KV_RANK, HEADS * D_V)),
        _resident((1, D_HEAD_PAD)), _resident((1, D_HEAD_PAD)),
    ]
    out_shape = [
        jax.ShapeDtypeStruct((bsz, s, D_MODEL), F32),
        jax.ShapeDtypeStruct((bsz, HEADS, s, D_HEAD_PAD), BF16),
        jax.ShapeDtypeStruct((bsz, HEADS, s, D_HEAD_PAD), BF16),
        jax.ShapeDtypeStruct((bsz, s, HEADS * D_V), BF16),
        jax.ShapeDtypeStruct((bsz, s, LRU_WIDTH), BF16),
        jax.ShapeDtypeStruct((bsz, s, LRU_WIDTH), BF16),
    ]
    head_blk = lambda b, i: (b, 0, i, 0)
    out_specs = [
        pl.BlockSpec((1, tm, D_MODEL), tok),
        pl.BlockSpec((1, HEADS, tm, D_HEAD_PAD), head_blk),
        pl.BlockSpec((1, HEADS, tm, D_HEAD_PAD), head_blk),
        pl.BlockSpec((1, tm, HEADS * D_V), tok),
        pl.BlockSpec((1, tm, LRU_WIDTH), tok),
        pl.BlockSpec((1, tm, LRU_WIDTH), tok),
    ]
    return pl.pallas_call(
        _front_kernel,
        grid=grid, in_specs=in_specs, out_specs=out_specs, out_shape=out_shape,
        scratch_shapes=[pltpu.VMEM((tm, D_FF), BF16)],
        compiler_params=pltpu.CompilerParams(
            dimension_semantics=("parallel", "parallel"), vmem_limit_bytes=VMEM_LIMIT_BIG),
        name="front",
    )(x, rope_tab, w["n1"], w["wg1"], w["wu1"], w["wd1"], w["nmix"], w["win"],
      w["nq"], w["wuq"], w["nkv"], w["wuk"], w["wuv"], w["qhn"], w["khn"])


def _attn_kernel(q_ref, k_ref, v_ref, km_ref, vm_ref, o_ref):
    qi = pl.program_id(2)
    q = q_ref[0, 0]

    s = _dot_nt(q, km_ref[0, 0])
    m = jnp.max(s, axis=-1, keepdims=True)
    p = jnp.exp(s - m)
    l = jnp.sum(p, axis=-1, keepdims=True)
    acc = _dot(p.astype(BF16), vm_ref[0])

    def update(carry, s, vj):
        m, l, acc = carry
        m_new = jnp.maximum(m, jnp.max(s, axis=-1, keepdims=True))
        alpha = jnp.exp(m - m_new)
        p = jnp.exp(s - m_new)
        l = alpha * l + jnp.sum(p, axis=-1, keepdims=True)
        acc = alpha * acc + _dot(p.astype(BF16), vj)
        return m_new, l, acc

    def body(j, carry):
        start = pl.multiple_of(j * TK, TK)
        kj = k_ref[0, 0, pl.ds(start, TK), :]
        vj = v_ref[0, pl.ds(start, TK), :]
        return update(carry, _dot_nt(q, kj), vj)

    carry = lax.fori_loop(0, qi, body, (m, l, acc))

    start = pl.multiple_of(qi * TK, TK)
    kj = k_ref[0, 0, pl.ds(start, TK), :]
    vj = v_ref[0, pl.ds(start, TK), :]
    s = _dot_nt(q, kj)
    row = lax.broadcasted_iota(jnp.int32, (TQ, TK), 0) // CHUNK
    col = lax.broadcasted_iota(jnp.int32, (TQ, TK), 1) // CHUNK
    s = jnp.where(col <= row, s, NEG_INF)
    m, l, acc = update(carry, s, vj)
    o_ref[0] = (acc / l).astype(o_ref.dtype)


def _attention(q, k, v, k_meta, v_meta):
    bsz, _, s, _ = q.shape
    grid = (bsz, HEADS, s // TQ)
    in_specs = [
        pl.BlockSpec((1, 1, TQ, D_HEAD_PAD), lambda b, h, i: (b, h, i, 0)),
        pl.BlockSpec((1, 1, s, D_HEAD_PAD), lambda b, h, i: (b, h, 0, 0)),
        pl.BlockSpec((1, s, D_V), lambda b, h, i: (b, 0, h)),
        pl.BlockSpec((1, 1, N_META, D_HEAD_PAD), lambda b, h, i: (0, h, 0, 0)),
        pl.BlockSpec((1, N_META, D_V), lambda b, h, i: (0, 0, h)),
    ]
    return pl.pallas_call(
        _attn_kernel,
        grid=grid, in_specs=in_specs,
        out_specs=pl.BlockSpec((1, TQ, D_V), lambda b, h, i: (b, i, h)),
        out_shape=jax.ShapeDtypeStruct((bsz, s, HEADS * D_V), BF16),
        compiler_params=pltpu.CompilerParams(
            dimension_semantics=("parallel", "parallel", "arbitrary"),
            vmem_limit_bytes=VMEM_LIMIT_SMALL),
        name="attn",
    )(q, k, v, k_meta, v_meta)


def _lru_kernel(first, u_ref, g_ref, hin_ref, tin_ref, cw_ref, cb_ref, wa_ref, ba_ref, wx_ref, bx_ref,
                lam_ref, y_ref, hout_ref, tout_ref, xbuf, hstate):
    j = pl.program_id(1)
    ts = u_ref.shape[1]
    tail = xbuf.shape[0] - ts

    @pl.when(j == 0)
    def _():
        xbuf[0:tail] = tin_ref[...]
        hstate[...] = hin_ref[...]

    xbuf[tail:tail + ts] = u_ref[0].astype(F32)
    xc = cb_ref[...]
    for kk in range(CONV_W):
        xc = xc + cw_ref[kk:kk + 1, :] * xbuf[pl.ds(tail - (CONV_W - 1) + kk, ts), :]
    xbuf[0:tail] = xbuf[ts:ts + tail]

    xcb = xc.astype(BF16)

    def gate(w_ref, b_ref):
        parts = [_dot(xcb[:, m * GATE_GROUP:(m + 1) * GATE_GROUP], w_ref[m])
                 for m in range(LRU_WIDTH // GATE_GROUP)]
        return jax.nn.sigmoid(jnp.concatenate(parts, axis=1) + b_ref[...])

    r = gate(wa_ref, ba_ref)
    i = gate(wx_ref, bx_ref)
    neg_lam = -lam_ref[...]
    softplus = jnp.maximum(neg_lam, 0.0) + jnp.log1p(jnp.exp(-jnp.abs(neg_lam)))
    log_a = -C_RGLRU * r * softplus
    a = jnp.exp(log_a)
    mult = jnp.sqrt(-jnp.tanh(log_a) * (a * a + 1.0))
    row = lax.broadcasted_iota(jnp.int32, (ts, LRU_WIDTH), 0)
    if first:
        mult = jnp.where((row == 0) & (j == 0), 1.0, mult)
    b = mult * (i * xc)

    sh = 1
    while sh < ts:
        keep = row >= sh
        a_prev = jnp.where(keep, pltpu.roll(a, sh, 0), 1.0)
        b_prev = jnp.where(keep, pltpu.roll(b, sh, 0), 0.0)
        b = b + a * b_prev
        a = a * a_prev
        sh *= 2
    h = a * hstate[...] + b
    hstate[...] = h[ts - 1:ts, :]
    y_ref[0] = (h * jax.nn.gelu(g_ref[0].astype(F32))).astype(y_ref.dtype)
    hout_ref[...] = h[ts - 1:ts, :]
    tout_ref[...] = xbuf[0:tail]


def _lru(u, g, h_in, tail_in, w, ts, first):
    bsz, s, _ = u.shape
    grid = (bsz, s // ts)
    tok = lambda b, i: (b, i, 0)
    const2 = lambda b, i: (0, 0)
    const3 = lambda b, i: (0, 0, 0)
    ngrp = LRU_WIDTH // GATE_GROUP
    in_specs = [
        pl.BlockSpec((1, ts, LRU_WIDTH), tok),
        pl.BlockSpec((1, ts, LRU_WIDTH), tok),
        pl.BlockSpec((1, LRU_WIDTH), const2),
        pl.BlockSpec((8, LRU_WIDTH), const2),
        pl.BlockSpec((CONV_W, LRU_WIDTH), const2),
        pl.BlockSpec((1, LRU_WIDTH), const2),
        pl.BlockSpec((ngrp, GATE_GROUP, GATE_GROUP), const3),
        pl.BlockSpec((1, LRU_WIDTH), const2),
        pl.BlockSpec((ngrp, GATE_GROUP, GATE_GROUP), const3),
        pl.BlockSpec((1, LRU_WIDTH), const2),
        pl.BlockSpec((1, LRU_WIDTH), const2),
    ]
    out_shape = [
        jax.ShapeDtypeStruct((bsz, s, LRU_WIDTH), BF16),
        jax.ShapeDtypeStruct((bsz, 1, LRU_WIDTH), F32),
        jax.ShapeDtypeStruct((bsz, 8, LRU_WIDTH), F32),
    ]
    out_specs = [
        pl.BlockSpec((1, ts, LRU_WIDTH), tok),
        pl.BlockSpec((None, 1, LRU_WIDTH), lambda b, i: (b, 0, 0)),
        pl.BlockSpec((None, 8, LRU_WIDTH), lambda b, i: (b, 0, 0)),
    ]
    return pl.pallas_call(
        functools.partial(_lru_kernel, first),
        grid=grid, in_specs=in_specs, out_specs=out_specs, out_shape=out_shape,
        scratch_shapes=[pltpu.VMEM((ts + 8, LRU_WIDTH), F32), pltpu.VMEM((1, LRU_WIDTH), F32)],
        compiler_params=pltpu.CompilerParams(
            dimension_semantics=("parallel", "arbitrary"), vmem_limit_bytes=VMEM_LIMIT_SMALL),
        name="lru_meta" if first else "lru",
    )(u, g, h_in, tail_in, w["cw"], w["cb"], w["wa"], w["ba"], w["wx"], w["bx"], w["lam"])


def _back_kernel(h_ref, ym_ref, yl_ref, nao_ref, nlo_ref, wout_ref, n2_ref, wg_ref, wu_ref, wd_ref,
                 nf_ref, o_ref, a_scr):
    ym = _rms(ym_ref[0].astype(F32), nao_ref[...]).astype(BF16)
    yl = _rms(yl_ref[0].astype(F32), nlo_ref[...]).astype(BF16)
    half = HEADS * D_V
    h = h_ref[0] + _dot(ym, wout_ref[:half, :]) + _dot(yl, wout_ref[half:, :])
    h = _ffn_half(h, n2_ref, wg_ref, wu_ref, wd_ref, a_scr)
    o_ref[0] = _rms(h, nf_ref[...])


def _back(h1, y_mla, y_lru, w, tm):
    bsz, s, _ = h1.shape
    grid = (bsz, s // tm)
    tok = lambda b, i: (b, i, 0)
    in_specs = [
        pl.BlockSpec((1, tm, D_MODEL), tok),
        pl.BlockSpec((1, tm, HEADS * D_V), tok),
        pl.BlockSpec((1, tm, LRU_WIDTH), tok),
        _resident((1, HEADS * D_V)), _resident((1, LRU_WIDTH)),
        _resident((HEADS * D_V + LRU_WIDTH, D_MODEL)),
        _resident((1, D_MODEL)),
        _resident((D_MODEL, D_FF)), _resident((D_MODEL, D_FF)), _resident((D_FF, D_MODEL)),
        _resident((1, D_MODEL)),
    ]
    return pl.pallas_call(
        _back_kernel,
        grid=grid, in_specs=in_specs,
        out_specs=pl.BlockSpec((1, tm, D_MODEL), tok),
        out_shape=jax.ShapeDtypeStruct((bsz, s, D_MODEL), F32),
        scratch_shapes=[pltpu.VMEM((tm, D_FF), BF16)],
        compiler_params=pltpu.CompilerParams(
            dimension_semantics=("parallel", "parallel"), vmem_limit_bytes=VMEM_LIMIT_BIG),
        name="back",
    )(h1, y_mla, y_lru, w["nao"], w["nlo"], w["wout"], w["n2"], w["wg2"], w["wu2"], w["wd2"], w["nf"])


def _rope_table(pos):
    half = D_ROPE // 2
    inv_freq = ROPE_THETA ** (-jnp.arange(0, half, dtype=F32) / half)
    ang = pos.astype(F32)[:, None] * inv_freq[None, :]
    cos, sin = jnp.cos(ang), jnp.sin(ang)
    zeros = jnp.zeros_like(cos)
    pad = jnp.zeros((pos.shape[0], LANES - D_ROPE), F32)
    return jnp.stack([
        jnp.concatenate([cos, cos, pad], axis=1),
        jnp.concatenate([-sin, zeros, pad], axis=1),
        jnp.concatenate([zeros, sin, pad], axis=1),
    ])


def _block_diag_groups(wgt):
    per = GATE_GROUP // LRU_BLOCK
    w4 = wgt.reshape(LRU_BLOCKS // per, per, LRU_BLOCK, LRU_BLOCK)
    eye = jnp.eye(per, dtype=wgt.dtype)
    return jnp.einsum('mnij,nk->mnikj', w4, eye).reshape(LRU_BLOCKS // per, GATE_GROUP, GATE_GROUP)


def kernel(x, meta_tokens, ffn1_norm, ffn1_w_gate, ffn1_w_up, ffn1_w_down, mix_norm, w_in, q_latent_norm, w_uq, kv_latent_norm, w_uk, w_uv, q_head_norm, k_head_norm, conv_w, conv_b, gate_a_w, gate_a_b, gate_x_w, gate_x_b, lru_lambda, attn_out_norm, lru_out_norm, w_out, ffn2_norm, ffn2_w_gate, ffn2_w_up, ffn2_w_down, final_norm):
    l = 0
    row = lambda a: a[l].reshape(1, -1).astype(F32)
    bf = lambda a: a[l].astype(BF16)

    o2 = Q_RANK + KV_RANK
    o3 = o2 + D_ROPE
    win = w_in[l]
    win_p = jnp.concatenate(
        [win[:, :o3], jnp.zeros((D_MODEL, LANES - D_ROPE), win.dtype), win[:, o3:]], axis=1)
    wuq_p = jnp.pad(w_uq[l].reshape(Q_RANK, HEADS, D_QK),
                    ((0, 0), (0, 0), (0, D_HEAD_PAD - D_QK))).reshape(Q_RANK, HEADS * D_HEAD_PAD)
    pad_head = lambda a: jnp.pad(a[l].astype(F32), (0, D_HEAD_PAD - D_QK)).reshape(1, D_HEAD_PAD)
    w = {
        "n1": row(ffn1_norm), "wg1": bf(ffn1_w_gate), "wu1": bf(ffn1_w_up), "wd1": bf(ffn1_w_down),
        "nmix": row(mix_norm), "win": win_p.astype(BF16),
        "nq": row(q_latent_norm), "wuq": wuq_p.astype(BF16),
        "nkv": row(kv_latent_norm), "wuk": bf(w_uk), "wuv": bf(w_uv),
        "qhn": pad_head(q_head_norm), "khn": pad_head(k_head_norm),
        "cw": conv_w[l].astype(F32), "cb": row(conv_b),
        "wa": _block_diag_groups(gate_a_w[l]).astype(BF16), "ba": row(gate_a_b),
        "wx": _block_diag_groups(gate_x_w[l]).astype(BF16), "bx": row(gate_x_b),
        "lam": row(lru_lambda),
        "nao": row(attn_out_norm), "nlo": row(lru_out_norm), "wout": bf(w_out),
        "n2": row(ffn2_norm), "wg2": bf(ffn2_w_gate), "wu2": bf(ffn2_w_up), "wd2": bf(ffn2_w_down),
        "nf": row(final_norm),
    }

    meta = meta_tokens.astype(x.dtype)[None]
    rope_meta = _rope_table(jnp.arange(N_META))
    _, _, k_meta, v_meta, u_meta, g_meta = _front(meta, rope_meta, w, N_META)
    zero_h = jnp.zeros((1, LRU_WIDTH), F32)
    zero_tail = jnp.zeros((8, LRU_WIDTH), F32)
    _, h_meta, tail_meta = _lru(u_meta, g_meta, zero_h, zero_tail, w, N_META, True)

    rope_main = _rope_table(N_META + jnp.arange(SEQ))
    h1, q, k, v, u, g = _front(x, rope_main, w, TM)
    y_mla = _attention(q, k, v, k_meta, v_meta)
    y_lru, _, _ = _lru(u, g, h_meta[0], tail_meta[0], w, TS, False)
    return _back(h1, y_mla, y_lru, w, TM)
```

```python
import functools
import math

import jax
import jax.numpy as jnp
from jax import lax
from jax.experimental import pallas as pl
from jax.experimental.pallas import tpu as pltpu

D_MODEL = 1024
SEQ = 2048
N_META = 16
CHUNK = 64
HEADS = 4
D_NOPE = 128
D_ROPE = 64
D_QK = D_NOPE + D_ROPE
D_V = 128
KV_RANK = 256
Q_RANK = 384
ROPE_THETA = 10000.0
LRU_WIDTH = 512
LRU_BLOCKS = 8
LRU_BLOCK = 64
CONV_W = 4
C_RGLRU = 8.0
D_FF = 2816
EPS = 1e-6
NEG_INF = -1e30

LANES = 128
D_HEAD_PAD = 2 * LANES
Z_WIDTH = Q_RANK + KV_RANK + LANES + 2 * LRU_WIDTH
GATE_GROUP = 256

TM = 256
FF_CHUNK = 256
TQ = 256
TK = 256
LRU_TB = 16
LRU_TT = 128
VMEM_LIMIT_BIG = 56 * 1024 * 1024
VMEM_LIMIT_SMALL = 32 * 1024 * 1024

F32 = jnp.float32
BF16 = jnp.bfloat16


def _rms(x, g):
    ms = jnp.mean(x * x, axis=-1, keepdims=True)
    return x * lax.rsqrt(ms + EPS) * g


def _dot(a, b):
    return jnp.dot(a, b, preferred_element_type=F32)


def _dot_nt(a, b):
    return lax.dot_general(a, b, (((1,), (1,)), ((), ())), preferred_element_type=F32)


def _ffn_half(h, g_ref, wg_ref, wu_ref, wd_ref, a_ref):
    u = _rms(h, g_ref[...]).astype(BF16)
    for c in range(D_FF // FF_CHUNK):
        sl = slice(c * FF_CHUNK, (c + 1) * FF_CHUNK)
        gate = _dot(u, wg_ref[:, sl])
        up = _dot(u, wu_ref[:, sl])
        a_ref[:, sl] = (jax.nn.silu(gate) * up).astype(BF16)
    return h + 0.5 * _dot(a_ref[...], wd_ref[...])


def _front_kernel(x_ref, rope_ref, n1_ref, wg_ref, wu_ref, wd_ref, nmix_ref, win_ref,
                  nq_ref, wuq_ref, nkv_ref, wuk_ref, wuv_ref, qhn_ref, khn_ref,
                  h_ref, q_ref, k_ref, v_ref, u_ref, g_ref, a_scr):
    h = _ffn_half(x_ref[0], n1_ref, wg_ref, wu_ref, wd_ref, a_scr)
    h_ref[0] = h
    z = _dot(_rms(h, nmix_ref[...]).astype(BF16), win_ref[...])
    o1 = Q_RANK
    o2 = o1 + KV_RANK
    o3 = o2 + LANES
    o4 = o3 + LRU_WIDTH
    c_q, c_kv, k_r = z[:, :o1], z[:, o1:o2], z[:, o2:o3]
    u_ref[0] = z[:, o3:o4].astype(u_ref.dtype)
    g_ref[0] = z[:, o4:].astype(g_ref.dtype)

    cos, sin_lo, sin_hi = rope_ref[0], rope_ref[1], rope_ref[2]

    def rope(x):
        return x * cos + pltpu.roll(x, LANES - D_ROPE // 2, 1) * sin_lo + pltpu.roll(x, D_ROPE // 2, 1) * sin_hi

    q_all = _dot(_rms(c_q, nq_ref[...]).astype(BF16), wuq_ref[...])
    ckv = _rms(c_kv, nkv_ref[...]).astype(BF16)
    k_nope = _dot(ckv, wuk_ref[...])
    v_ref[0] = _dot(ckv, wuv_ref[...]).astype(v_ref.dtype)

    qhn = qhn_ref[...]
    khn = khn_ref[...]
    kr_roped = rope(k_r * khn[:, LANES:])
    kr_ss = jnp.sum(k_r * k_r, axis=-1, keepdims=True)
    scale = math.log2(math.e) / math.sqrt(D_QK)
    for hd in range(HEADS):
        qh = q_all[:, hd * D_HEAD_PAD:(hd + 1) * D_HEAD_PAD]
        q_rinv = lax.rsqrt(jnp.sum(qh * qh, axis=-1, keepdims=True) / D_QK + EPS)
        qn = qh * q_rinv * qhn
        q_out = jnp.concatenate([qn[:, :LANES], rope(qn[:, LANES:])], axis=1) * scale
        q_ref[0, hd] = q_out.astype(q_ref.dtype)
        kn = k_nope[:, hd * D_NOPE:(hd + 1) * D_NOPE]
        k_rinv = lax.rsqrt((jnp.sum(kn * kn, axis=-1, keepdims=True) + kr_ss) / D_QK + EPS)
        k_out = jnp.concatenate([kn * k_rinv * khn[:, :LANES], kr_roped * k_rinv], axis=1)
        k_ref[0, hd] = k_out.astype(k_ref.dtype)


def _resident(shape):
    nd = len(shape)
    return pl.BlockSpec(shape, lambda *_: (0,) * nd, pipeline_mode=pl.Buffered(1))


def _front(x, rope_tab, w, tm):
    bsz, s, _ = x.shape
    ns = s // tm
    grid = (bsz, ns)
    tok = lambda b, i: (b, i, 0)
    in_specs = [
        pl.BlockSpec((1, tm, D_MODEL), tok),
        pl.BlockSpec((3, tm, LANES), lambda b, i: (0, i, 0)),
        _resident((1, D_MODEL)),
        _resident((D_MODEL, D_FF)), _resident((D_MODEL, D_FF)), _resident((D_FF, D_MODEL)),
        _resident((1, D_MODEL)), _resident((D_MODEL, Z_WIDTH)),
        _resident((1, Q_RANK)), _resident((Q_RANK, HEADS * D_HEAD_PAD)),
        _resident((1, KV_RANK)), _resident((KV_RANK, HEADS * D_NOPE)), _resident((KV_RANK, HEADS * D_V)),
        _resident((1, D_HEAD_PAD)), _resident((1, D_HEAD_PAD)),
    ]
    out_shape = [
        jax.ShapeDtypeStruct((bsz, s, D_MODEL), F32),
        jax.ShapeDtypeStruct((bsz, HEADS, s, D_HEAD_PAD), BF16),
        jax.ShapeDtypeStruct((bsz, HEADS, s, D_HEAD_PAD), BF16),
        jax.ShapeDtypeStruct((bsz, s, HEADS * D_V), BF16),
        jax.ShapeDtypeStruct((bsz, s, LRU_WIDTH), BF16),
        jax.ShapeDtypeStruct((bsz, s, LRU_WIDTH), BF16),
    ]
    head_blk = lambda b, i: (b, 0, i, 0)
    out_specs = [
        pl.BlockSpec((1, tm, D_MODEL), tok),
        pl.BlockSpec((1, HEADS, tm, D_HEAD_PAD), head_blk),
        pl.BlockSpec((1, HEADS, tm, D_HEAD_PAD), head_blk),
        pl.BlockSpec((1, tm, HEADS * D_V), tok),
        pl.BlockSpec((1, tm, LRU_WIDTH), tok),
        pl.BlockSpec((1, tm, LRU_WIDTH), tok),
    ]
    return pl.pallas_call(
        _front_kernel,
        grid=grid, in_specs=in_specs, out_specs=out_specs, out_shape=out_shape,
        scratch_shapes=[pltpu.VMEM((tm, D_FF), BF16)],
        compiler_params=pltpu.CompilerParams(
            dimension_semantics=("parallel", "parallel"), vmem_limit_bytes=VMEM_LIMIT_BIG),
        name="front",
    )(x, rope_tab, w["n1"], w["wg1"], w["wu1"], w["wd1"], w["nmix"], w["win"],
      w["nq"], w["wuq"], w["nkv"], w["wuk"], w["wuv"], w["qhn"], w["khn"])


def _attn_kernel(q_ref, k_ref, v_ref, km_ref, vm_ref, o_ref):
    seq = q_ref.shape[2]
    km = km_ref[0, 0]
    vm = vm_ref[0]
    row = lax.broadcasted_iota(jnp.int32, (TQ, TQ), 0) // CHUNK
    col = lax.broadcasted_iota(jnp.int32, (TQ, TQ), 1) // CHUNK
    diag_mask = col <= row
    for i in range(seq // TQ):
        lo, hi = i * TQ, (i + 1) * TQ
        q = q_ref[0, 0, lo:hi, :]
        s_meta = _dot_nt(q, km)
        s_diag = jnp.where(diag_mask, _dot_nt(q, k_ref[0, 0, lo:hi, :]), NEG_INF)
        m = jnp.maximum(jnp.max(s_meta, axis=-1, keepdims=True),
                        jnp.max(s_diag, axis=-1, keepdims=True))
        if i > 0:
            s_past = _dot_nt(q, k_ref[0, 0, :lo, :])
            m = jnp.maximum(m, jnp.max(s_past, axis=-1, keepdims=True))
        p_meta = jnp.exp2(s_meta - m)
        p_diag = jnp.exp2(s_diag - m)
        l = jnp.sum(p_meta, axis=-1, keepdims=True) + jnp.sum(p_diag, axis=-1, keepdims=True)
        acc = _dot(p_meta.astype(BF16), vm) + _dot(p_diag.astype(BF16), v_ref[0, lo:hi, :])
        if i > 0:
            p_past = jnp.exp2(s_past - m)
            l = l + jnp.sum(p_past, axis=-1, keepdims=True)
            acc = acc + _dot(p_past.astype(BF16), v_ref[0, :lo, :])
        o_ref[0, lo:hi, :] = (acc / l).astype(o_ref.dtype)


def _attention(q, k, v, k_meta, v_meta):
    bsz, _, s, _ = q.shape
    grid = (bsz, HEADS)
    in_specs = [
        pl.BlockSpec((1, 1, s, D_HEAD_PAD), lambda b, h: (b, h, 0, 0)),
        pl.BlockSpec((1, 1, s, D_HEAD_PAD), lambda b, h: (b, h, 0, 0)),
        pl.BlockSpec((1, s, D_V), lambda b, h: (b, 0, h)),
        pl.BlockSpec((1, 1, N_META, D_HEAD_PAD), lambda b, h: (0, h, 0, 0)),
        pl.BlockSpec((1, N_META, D_V), lambda b, h: (0, 0, h)),
    ]
    return pl.pallas_call(
        _attn_kernel,
        grid=grid, in_specs=in_specs,
        out_specs=pl.BlockSpec((1, s, D_V), lambda b, h: (b, 0, h)),
        out_shape=jax.ShapeDtypeStruct((bsz, s, HEADS * D_V), BF16),
        compiler_params=pltpu.CompilerParams(
            dimension_semantics=("parallel", "parallel"),
            vmem_limit_bytes=VMEM_LIMIT_SMALL),
        name="attn",
    )(q, k, v, k_meta, v_meta)


def _sigmoid(x):
    return 0.5 * jnp.tanh(0.5 * x) + 0.5


def _lru_kernel(first, u_ref, g_ref, hin_ref, tin_ref, perm_ref, cw_ref, cb_ref, wa_ref, ba_ref,
                wx_ref, bx_ref, lam_ref, y_ref, hout_ref, tout_ref, ubuf, hstate):
    j = pl.program_id(0)
    bsz, tt, _ = u_ref.shape
    rows = LRU_TB * bsz
    tail = (CONV_W - 1) * bsz

    @pl.when(j == 0)
    def _():
        ubuf[0:tail] = tin_ref[...]
        hstate[...] = hin_ref[...]

    perm = perm_ref[...]
    neg_lam = -lam_ref[...]
    softplus = jnp.maximum(neg_lam, 0.0) + jnp.log1p(jnp.exp(-jnp.abs(neg_lam)))
    decay = -C_RGLRU * softplus
    h = hstate[...]
    for k in range(tt // LRU_TB):
        t0 = k * LRU_TB
        uk = jnp.concatenate([u_ref[b, t0:t0 + LRU_TB, :] for b in range(bsz)], axis=0)
        gk = jnp.concatenate([g_ref[b, t0:t0 + LRU_TB, :] for b in range(bsz)], axis=0)
        ubuf[tail:tail + rows] = _dot(perm, uk)
        gp = _dot(perm, gk)
        xc = cb_ref[...]
        for kk in range(CONV_W):
            xc = xc + cw_ref[kk:kk + 1, :] * ubuf[kk * bsz:kk * bsz + rows, :]
        ubuf[0:tail] = ubuf[rows:rows + tail]
        xcb = xc.astype(BF16)

        def gate(w_ref, b_ref):
            parts = [_dot(xcb[:, m * GATE_GROUP:(m + 1) * GATE_GROUP], w_ref[m])
                     for m in range(LRU_WIDTH // GATE_GROUP)]
            return _sigmoid(jnp.concatenate(parts, axis=1) + b_ref[...])

        log_a = decay * gate(wa_ref, ba_ref)
        a = jnp.exp(log_a)
        mult = jnp.sqrt(-jnp.tanh(log_a) * (a * a + 1.0))
        if first and k == 0:
            row = lax.broadcasted_iota(jnp.int32, (rows, LRU_WIDTH), 0)
            mult = jnp.where((row < bsz) & (j == 0), 1.0, mult)
        bt = mult * (gate(wx_ref, bx_ref) * xc)
        hs = []
        for t in range(LRU_TB):
            sl = slice(t * bsz, (t + 1) * bsz)
            h = a[sl] * h + bt[sl]
            hs.append(h)
        yp = (jnp.concatenate(hs, axis=0) * jax.nn.gelu(gp)).astype(BF16)
        y = _dot(perm, yp).astype(y_ref.dtype)
        for b in range(bsz):
            y_ref[b, t0:t0 + LRU_TB, :] = y[b * LRU_TB:(b + 1) * LRU_TB]
    hstate[...] = h
    hout_ref[...] = h
    tout_ref[...] = ubuf[0:tail]


def _lru(u, g, h_in, tail_in, w, tt, first):
    bsz, s, _ = u.shape
    rows = LRU_TB * bsz
    tail = (CONV_W - 1) * bsz
    tok = lambda i: (0, i, 0)
    const2 = lambda i: (0, 0)
    const3 = lambda i: (0, 0, 0)
    ngrp = LRU_WIDTH // GATE_GROUP
    in_specs = [
        pl.BlockSpec((bsz, tt, LRU_WIDTH), tok),
        pl.BlockSpec((bsz, tt, LRU_WIDTH), tok),
        pl.BlockSpec((bsz, LRU_WIDTH), const2),
        pl.BlockSpec((tail, LRU_WIDTH), const2),
        pl.BlockSpec((rows, rows), const2),
        pl.BlockSpec((CONV_W, LRU_WIDTH), const2),
        pl.BlockSpec((1, LRU_WIDTH), const2),
        pl.BlockSpec((ngrp, GATE_GROUP, GATE_GROUP), const3),
        pl.BlockSpec((1, LRU_WIDTH), const2),
        pl.BlockSpec((ngrp, GATE_GROUP, GATE_GROUP), const3),
        pl.BlockSpec((1, LRU_WIDTH), const2),
        pl.BlockSpec((1, LRU_WIDTH), const2),
    ]
    out_shape = [
        jax.ShapeDtypeStruct((bsz, s, LRU_WIDTH), BF16),
        jax.ShapeDtypeStruct((bsz, LRU_WIDTH), F32),
        jax.ShapeDtypeStruct((tail, LRU_WIDTH), F32),
    ]
    out_specs = [
        pl.BlockSpec((bsz, tt, LRU_WIDTH), tok),
        pl.BlockSpec((bsz, LRU_WIDTH), const2),
        pl.BlockSpec((tail, LRU_WIDTH), const2),
    ]
    assert LRU_TB == bsz
    eye = jnp.eye(rows, dtype=BF16).reshape(bsz, LRU_TB, rows)
    perm = eye.transpose(1, 0, 2).reshape(rows, rows)
    return pl.pallas_call(
        functools.partial(_lru_kernel, first),
        grid=(s // tt,), in_specs=in_specs, out_specs=out_specs, out_shape=out_shape,
        scratch_shapes=[pltpu.VMEM((tail + rows, LRU_WIDTH), F32), pltpu.VMEM((bsz, LRU_WIDTH), F32)],
        compiler_params=pltpu.CompilerParams(
            dimension_semantics=("arbitrary",), vmem_limit_bytes=VMEM_LIMIT_SMALL),
        name="lru_meta" if first else "lru",
    )(u, g, h_in, tail_in, perm, w["cw"], w["cb"], w["wa"], w["ba"], w["wx"], w["bx"], w["lam"])


def _back_kernel(h_ref, ym_ref, yl_ref, nao_ref, nlo_ref, wout_ref, n2_ref, wg_ref, wu_ref, wd_ref,
                 nf_ref, o_ref, a_scr):
    ym = _rms(ym_ref[0].astype(F32), nao_ref[...]).astype(BF16)
    yl = _rms(yl_ref[0].astype(F32), nlo_ref[...]).astype(BF16)
    half = HEADS * D_V
    h = h_ref[0] + _dot(ym, wout_ref[:half, :]) + _dot(yl, wout_ref[half:, :])
    h = _ffn_half(h, n2_ref, wg_ref, wu_ref, wd_ref, a_scr)
    o_ref[0] = _rms(h, nf_ref[...])


def _back(h1, y_mla, y_lru, w, tm):
    bsz, s, _ = h1.shape
    grid = (bsz, s // tm)
    tok = lambda b, i: (b, i, 0)
    in_specs = [
        pl.BlockSpec((1, tm, D_MODEL), tok),
        pl.BlockSpec((1, tm, HEADS * D_V), tok),
        pl.BlockSpec((1, tm, LRU_WIDTH), tok),
        _resident((1, HEADS * D_V)), _resident((1, LRU_WIDTH)),
        _resident((HEADS * D_V + LRU_WIDTH, D_MODEL)),
        _resident((1, D_MODEL)),
        _resident((D_MODEL, D_FF)), _resident((D_MODEL, D_FF)), _resident((D_FF, D_MODEL)),
        _resident((1, D_MODEL)),
    ]
    return pl.pallas_call(
        _back_kernel,
        grid=grid, in_specs=in_specs,
        out_specs=pl.BlockSpec((1, tm, D_MODEL), tok),
        out_shape=jax.ShapeDtypeStruct((bsz, s, D_MODEL), F32),
        scratch_shapes=[pltpu.VMEM((tm, D_FF), BF16)],
        compiler_params=pltpu.CompilerParams(
            dimension_semantics=("parallel", "parallel"), vmem_limit_bytes=VMEM_LIMIT_BIG),
        name="back",
    )(h1, y_mla, y_lru, w["nao"], w["nlo"], w["wout"], w["n2"], w["wg2"], w["wu2"], w["wd2"], w["nf"])


def _rope_table(pos):
    half = D_ROPE // 2
    inv_freq = ROPE_THETA ** (-jnp.arange(0, half, dtype=F32) / half)
    ang = pos.astype(F32)[:, None] * inv_freq[None, :]
    cos, sin = jnp.cos(ang), jnp.sin(ang)
    zeros = jnp.zeros_like(cos)
    pad = jnp.zeros((pos.shape[0], LANES - D_ROPE), F32)
    return jnp.stack([
        jnp.concatenate([cos, cos, pad], axis=1),
        jnp.concatenate([-sin, zeros, pad], axis=1),
        jnp.concatenate([zeros, sin, pad], axis=1),
    ])


def _block_diag_groups(wgt):
    per = GATE_GROUP // LRU_BLOCK
    w4 = wgt.reshape(LRU_BLOCKS // per, per, LRU_BLOCK, LRU_BLOCK)
    eye = jnp.eye(per, dtype=wgt.dtype)
    return jnp.einsum('mnij,nk->mnikj', w4, eye).reshape(LRU_BLOCKS // per, GATE_GROUP, GATE_GROUP)


def kernel(x, meta_tokens, ffn1_norm, ffn1_w_gate, ffn1_w_up, ffn1_w_down, mix_norm, w_in, q_latent_norm, w_uq, kv_latent_norm, w_uk, w_uv, q_head_norm, k_head_norm, conv_w, conv_b, gate_a_w, gate_a_b, gate_x_w, gate_x_b, lru_lambda, attn_out_norm, lru_out_norm, w_out, ffn2_norm, ffn2_w_gate, ffn2_w_up, ffn2_w_down, final_norm):
    l = 0
    row = lambda a: a[l].reshape(1, -1).astype(F32)
    bf = lambda a: a[l].astype(BF16)

    o2 = Q_RANK + KV_RANK
    o3 = o2 + D_ROPE
    win = w_in[l]
    win_p = jnp.concatenate(
        [win[:, :o3], jnp.zeros((D_MODEL, LANES - D_ROPE), win.dtype), win[:, o3:]], axis=1)
    wuq_p = jnp.pad(w_uq[l].reshape(Q_RANK, HEADS, D_QK),
                    ((0, 0), (0, 0), (0, D_HEAD_PAD - D_QK))).reshape(Q_RANK, HEADS * D_HEAD_PAD)
    pad_head = lambda a: jnp.pad(a[l].astype(F32), (0, D_HEAD_PAD - D_QK)).reshape(1, D_HEAD_PAD)
    w = {
        "n1": row(ffn1_norm), "wg1": bf(ffn1_w_gate), "wu1": bf(ffn1_w_up), "wd1": bf(ffn1_w_down),
        "nmix": row(mix_norm), "win": win_p.astype(BF16),
        "nq": row(q_latent_norm), "wuq": wuq_p.astype(BF16),
        "nkv": row(kv_latent_norm), "wuk": bf(w_uk), "wuv": bf(w_uv),
        "qhn": pad_head(q_head_norm), "khn": pad_head(k_head_norm),
        "cw": conv_w[l].astype(F32), "cb": row(conv_b),
        "wa": _block_diag_groups(gate_a_w[l]).astype(BF16), "ba": row(gate_a_b),
        "wx": _block_diag_groups(gate_x_w[l]).astype(BF16), "bx": row(gate_x_b),
        "lam": row(lru_lambda),
        "nao": row(attn_out_norm), "nlo": row(lru_out_norm), "wout": bf(w_out),
        "n2": row(ffn2_norm), "wg2": bf(ffn2_w_gate), "wu2": bf(ffn2_w_up), "wd2": bf(ffn2_w_down),
        "nf": row(final_norm),
    }

    meta = meta_tokens.astype(x.dtype)[None]
    rope_meta = _rope_table(jnp.arange(N_META))
    _, _, k_meta, v_meta, u_meta, g_meta = _front(meta, rope_meta, w, N_META)
    bsz = x.shape[0]
    rep = lambda a: jnp.broadcast_to(a, (bsz,) + a.shape[1:])
    zero_h = jnp.zeros((bsz, LRU_WIDTH), F32)
    zero_tail = jnp.zeros(((CONV_W - 1) * bsz, LRU_WIDTH), F32)
    _, h_meta, tail_meta = _lru(rep(u_meta), rep(g_meta), zero_h, zero_tail, w, N_META, True)

    rope_main = _rope_table(N_META + jnp.arange(SEQ))
    h1, q, k, v, u, g = _front(x, rope_main, w, TM)
    y_mla = _attention(q, k, v, k_meta, v_meta)
    y_lru, _, _ = _lru(u, g, h_meta, tail_meta, w, LRU_TT, False)
    return _back(h1, y_mla, y_lru, w, TM)
```

```python
import functools
import math

import jax
import jax.numpy as jnp
from jax import lax
from jax.experimental import pallas as pl
from jax.experimental.pallas import tpu as pltpu

D_MODEL = 1024
SEQ = 2048
N_META = 16
CHUNK = 64
HEADS = 4
D_NOPE = 128
D_ROPE = 64
D_QK = D_NOPE + D_ROPE
D_V = 128
KV_RANK = 256
Q_RANK = 384
ROPE_THETA = 10000.0
LRU_WIDTH = 512
LRU_BLOCKS = 8
LRU_BLOCK = 64
CONV_W = 4
C_RGLRU = 8.0
D_FF = 2816
EPS = 1e-6
NEG_INF = -1e30

LANES = 128
D_HEAD_PAD = 2 * LANES
Z_WIDTH = Q_RANK + KV_RANK + LANES + 2 * LRU_WIDTH
GATE_GROUP = 256

TM = 256
FF_CHUNK = 256
TQ = 256
ATTN_HEADS_PER_STEP = 2
LRU_TB = 16
LRU_TT = 128
VMEM_LIMIT_BIG = 56 * 1024 * 1024
VMEM_LIMIT_SMALL = 32 * 1024 * 1024

F32 = jnp.float32
BF16 = jnp.bfloat16


def _rms(x, g):
    ms = jnp.mean(x * x, axis=-1, keepdims=True)
    return x * lax.rsqrt(ms + EPS) * g


def _dot(a, b):
    return jnp.dot(a, b, preferred_element_type=F32)


def _dot_nt(a, b):
    return lax.dot_general(a, b, (((1,), (1,)), ((), ())), preferred_element_type=F32)


def _ffn_half(h, g_ref, wg_ref, wu_ref, wd_ref, a_ref):
    u = _rms(h, g_ref[...]).astype(BF16)
    for c in range(D_FF // FF_CHUNK):
        sl = slice(c * FF_CHUNK, (c + 1) * FF_CHUNK)
        gate = _dot(u, wg_ref[:, sl])
        up = _dot(u, wu_ref[:, sl])
        a_ref[:, sl] = (jax.nn.silu(gate) * up).astype(BF16)
    return h + 0.5 * _dot(a_ref[...], wd_ref[...])


def _front_kernel(x_ref, rope_ref, n1_ref, wg_ref, wu_ref, wd_ref, nmix_ref, win_ref,
                  nq_ref, wuq_ref, nkv_ref, wuk_ref, wuv_ref, qhn_ref, khn_ref,
                  h_ref, q_ref, k_ref, v_ref, u_ref, g_ref, a_scr):
    h = _ffn_half(x_ref[0], n1_ref, wg_ref, wu_ref, wd_ref, a_scr)
    h_ref[0] = h
    z = _dot(_rms(h, nmix_ref[...]).astype(BF16), win_ref[...])
    o1 = Q_RANK
    o2 = o1 + KV_RANK
    o3 = o2 + LANES
    o4 = o3 + LRU_WIDTH
    c_q, c_kv, k_r = z[:, :o1], z[:, o1:o2], z[:, o2:o3]
    u_ref[0] = z[:, o3:o4].astype(u_ref.dtype)
    g_ref[0] = z[:, o4:].astype(g_ref.dtype)

    cos, sin_lo, sin_hi = rope_ref[0], rope_ref[1], rope_ref[2]

    def rope(x):
        return x * cos + pltpu.roll(x, LANES - D_ROPE // 2, 1) * sin_lo + pltpu.roll(x, D_ROPE // 2, 1) * sin_hi

    q_all = _dot(_rms(c_q, nq_ref[...]).astype(BF16), wuq_ref[...])
    ckv = _rms(c_kv, nkv_ref[...]).astype(BF16)
    k_nope = _dot(ckv, wuk_ref[...])
    v_ref[0] = _dot(ckv, wuv_ref[...]).astype(v_ref.dtype)

    qhn = qhn_ref[...]
    khn = khn_ref[...]
    kr_roped = rope(k_r * khn[:, LANES:])
    kr_ss = jnp.sum(k_r * k_r, axis=-1, keepdims=True)
    scale = math.log2(math.e) / math.sqrt(D_QK)
    for hd in range(HEADS):
        qh = q_all[:, hd * D_HEAD_PAD:(hd + 1) * D_HEAD_PAD]
        q_rinv = lax.rsqrt(jnp.sum(qh * qh, axis=-1, keepdims=True) / D_QK + EPS)
        qn = qh * q_rinv * qhn
        q_out = jnp.concatenate([qn[:, :LANES], rope(qn[:, LANES:])], axis=1) * scale
        q_ref[0, hd] = q_out.astype(q_ref.dtype)
        kn = k_nope[:, hd * D_NOPE:(hd + 1) * D_NOPE]
        k_rinv = lax.rsqrt((jnp.sum(kn * kn, axis=-1, keepdims=True) + kr_ss) / D_QK + EPS)
        k_out = jnp.concatenate([kn * k_rinv * khn[:, :LANES], kr_roped * k_rinv], axis=1)
        k_ref[0, hd] = k_out.astype(k_ref.dtype)


def _resident(shape):
    nd = len(shape)
    return pl.BlockSpec(shape, lambda *_: (0,) * nd, pipeline_mode=pl.Buffered(1))


def _front(x, rope_tab, w, tm):
    bsz, s, _ = x.shape
    ns = s // tm
    grid = (bsz, ns)
    tok = lambda b, i: (b, i, 0)
    in_specs = [
        pl.BlockSpec((1, tm, D_MODEL), tok),
        pl.BlockSpec((3, tm, LANES), lambda b, i: (0, i, 0)),
        _resident((1, D_MODEL)),
        _resident((D_MODEL, D_FF)), _resident((D_MODEL, D_FF)), _resident((D_FF, D_MODEL)),
        _resident((1, D_MODEL)), _resident((D_MODEL, Z_WIDTH)),
        _resident((1, Q_RANK)), _resident((Q_RANK, HEADS * D_HEAD_PAD)),
        _resident((1, KV_RANK)), _resident((KV_RANK, HEADS * D_NOPE)), _resident((KV_RANK, HEADS * D_V)),
        _resident((1, D_HEAD_PAD)), _resident((1, D_HEAD_PAD)),
    ]
    out_shape = [
        jax.ShapeDtypeStruct((bsz, s, D_MODEL), F32),
        jax.ShapeDtypeStruct((bsz, HEADS, s, D_HEAD_PAD), BF16),
        jax.ShapeDtypeStruct((bsz, HEADS, s, D_HEAD_PAD), BF16),
        jax.ShapeDtypeStruct((bsz, s, HEADS * D_V), BF16),
        jax.ShapeDtypeStruct((bsz, s, LRU_WIDTH), BF16),
        jax.ShapeDtypeStruct((bsz, s, LRU_WIDTH), BF16),
    ]
    head_blk = lambda b, i: (b, 0, i, 0)
    out_specs = [
        pl.BlockSpec((1, tm, D_MODEL), tok),
        pl.BlockSpec((1, HEADS, tm, D_HEAD_PAD), head_blk),
        pl.BlockSpec((1, HEADS, tm, D_HEAD_PAD), head_blk),
        pl.BlockSpec((1, tm, HEADS * D_V), tok),
        pl.BlockSpec((1, tm, LRU_WIDTH), tok),
        pl.BlockSpec((1, tm, LRU_WIDTH), tok),
    ]
    return pl.pallas_call(
        _front_kernel,
        grid=grid, in_specs=in_specs, out_specs=out_specs, out_shape=out_shape,
        scratch_shapes=[pltpu.VMEM((tm, D_FF), BF16)],
        compiler_params=pltpu.CompilerParams(
            dimension_semantics=("parallel", "parallel"), vmem_limit_bytes=VMEM_LIMIT_BIG),
        name="front",
    )(x, rope_tab, w["n1"], w["wg1"], w["wu1"], w["wd1"], w["nmix"], w["win"],
      w["nq"], w["wuq"], w["nkv"], w["wuk"], w["wuv"], w["qhn"], w["khn"])


def _attn_kernel(q_ref, k_ref, v_ref, km_ref, vm_ref, o_ref, vx_scr, vmx_scr):
    hp, seq = q_ref.shape[1], q_ref.shape[2]
    for hd in range(hp):
        vcol = slice(hd * D_V, (hd + 1) * D_V)
        vx_scr[hd, :, :D_V] = v_ref[0, :, vcol]
        vx_scr[hd, :, D_V:] = jnp.ones((seq, D_V), BF16)
        vmx_scr[hd, :, :D_V] = vm_ref[0, :, vcol]
        vmx_scr[hd, :, D_V:] = jnp.ones((N_META, D_V), BF16)
    row = lax.broadcasted_iota(jnp.int32, (TQ, TQ), 0) // CHUNK
    col = lax.broadcasted_iota(jnp.int32, (TQ, TQ), 1) // CHUNK
    diag_mask = col <= row

    def scores(i, hd):
        lo, hi = i * TQ, (i + 1) * TQ
        q = q_ref[0, hd, lo:hi, :]
        s_meta = _dot_nt(q, km_ref[0, hd])
        s_diag = jnp.where(diag_mask, _dot_nt(q, k_ref[0, hd, lo:hi, :]), NEG_INF)
        s_past = _dot_nt(q, k_ref[0, hd, :lo, :]) if i > 0 else None
        return s_meta, s_diag, s_past

    def softmax(s_meta, s_diag, s_past):
        m = jnp.maximum(jnp.max(s_meta, axis=-1, keepdims=True),
                        jnp.max(s_diag, axis=-1, keepdims=True))
        if s_past is not None:
            m = jnp.maximum(m, jnp.max(s_past, axis=-1, keepdims=True))
        p_meta = jnp.exp2(s_meta - m).astype(BF16)
        p_diag = jnp.exp2(s_diag - m).astype(BF16)
        p_past = jnp.exp2(s_past - m).astype(BF16) if s_past is not None else None
        return p_meta, p_diag, p_past

    def weighted_values(i, hd, p_meta, p_diag, p_past):
        lo, hi = i * TQ, (i + 1) * TQ
        acc = _dot(p_meta, vmx_scr[hd]) + _dot(p_diag, vx_scr[hd, lo:hi, :])
        if p_past is not None:
            acc = acc + _dot(p_past, vx_scr[hd, :lo, :])
        o_ref[0, lo:hi, hd * D_V:(hd + 1) * D_V] = (acc[:, :D_V] / acc[:, D_V:]).astype(o_ref.dtype)

    items = [(i, hd) for i in range(seq // TQ) for hd in range(hp)]
    n_items = len(items)
    s_next = scores(*items[0])
    p_next = None
    for n in range(n_items + 1):
        s_cur, p_cur = s_next, p_next
        s_next = scores(*items[n + 1]) if n + 1 < n_items else None
        p_next = softmax(*s_cur) if n < n_items else None
        if n >= 1:
            weighted_values(*items[n - 1], *p_cur)


def _attention(q, k, v, k_meta, v_meta):
    bsz, _, s, _ = q.shape
    hp = ATTN_HEADS_PER_STEP
    grid = (bsz, HEADS // hp)
    in_specs = [
        pl.BlockSpec((1, hp, s, D_HEAD_PAD), lambda b, h: (b, h, 0, 0)),
        pl.BlockSpec((1, hp, s, D_HEAD_PAD), lambda b, h: (b, h, 0, 0)),
        pl.BlockSpec((1, s, hp * D_V), lambda b, h: (b, 0, h)),
        pl.BlockSpec((1, hp, N_META, D_HEAD_PAD), lambda b, h: (0, h, 0, 0)),
        pl.BlockSpec((1, N_META, hp * D_V), lambda b, h: (0, 0, h)),
    ]
    return pl.pallas_call(
        _attn_kernel,
        grid=grid, in_specs=in_specs,
        out_specs=pl.BlockSpec((1, s, hp * D_V), lambda b, h: (b, 0, h)),
        out_shape=jax.ShapeDtypeStruct((bsz, s, HEADS * D_V), BF16),
        scratch_shapes=[pltpu.VMEM((hp, s, 2 * D_V), BF16), pltpu.VMEM((hp, N_META, 2 * D_V), BF16)],
        compiler_params=pltpu.CompilerParams(
            dimension_semantics=("parallel", "parallel"),
            vmem_limit_bytes=VMEM_LIMIT_SMALL),
        name="attn",
    )(q, k, v, k_meta, v_meta)


def _sigmoid(x):
    return 0.5 * jnp.tanh(0.5 * x) + 0.5


def _lru_kernel(first, u_ref, g_ref, hin_ref, tin_ref, perm_ref, cw_ref, cb_ref, wa_ref, ba_ref,
                wx_ref, bx_ref, lam_ref, y_ref, hout_ref, tout_ref, ubuf, hstate):
    j = pl.program_id(0)
    bsz, tt, _ = u_ref.shape
    rows = LRU_TB * bsz
    tail = (CONV_W - 1) * bsz

    @pl.when(j == 0)
    def _():
        ubuf[0:tail] = tin_ref[...]
        hstate[...] = hin_ref[...]

    perm = perm_ref[...]
    neg_lam = -lam_ref[...]
    softplus = jnp.maximum(neg_lam, 0.0) + jnp.log1p(jnp.exp(-jnp.abs(neg_lam)))
    decay = -C_RGLRU * softplus
    h = hstate[...]
    for k in range(tt // LRU_TB):
        t0 = k * LRU_TB
        uk = jnp.concatenate([u_ref[b, t0:t0 + LRU_TB, :] for b in range(bsz)], axis=0)
        gk = jnp.concatenate([g_ref[b, t0:t0 + LRU_TB, :] for b in range(bsz)], axis=0)
        ubuf[tail:tail + rows] = _dot(perm, uk)
        gp = _dot(perm, gk)
        xc = cb_ref[...]
        for kk in range(CONV_W):
            xc = xc + cw_ref[kk:kk + 1, :] * ubuf[kk * bsz:kk * bsz + rows, :]
        ubuf[0:tail] = ubuf[rows:rows + tail]
        xcb = xc.astype(BF16)

        def gate(w_ref, b_ref):
            parts = [_dot(xcb[:, m * GATE_GROUP:(m + 1) * GATE_GROUP], w_ref[m])
                     for m in range(LRU_WIDTH // GATE_GROUP)]
            return _sigmoid(jnp.concatenate(parts, axis=1) + b_ref[...])

        log_a = decay * gate(wa_ref, ba_ref)
        a = jnp.exp(log_a)
        mult = jnp.sqrt(-jnp.tanh(log_a) * (a * a + 1.0))
        if first and k == 0:
            row = lax.broadcasted_iota(jnp.int32, (rows, LRU_WIDTH), 0)
            mult = jnp.where((row < bsz) & (j == 0), 1.0, mult)
        bt = mult * (gate(wx_ref, bx_ref) * xc)
        hs = []
        for t in range(LRU_TB):
            sl = slice(t * bsz, (t + 1) * bsz)
            h = a[sl] * h + bt[sl]
            hs.append(h)
        yp = (jnp.concatenate(hs, axis=0) * jax.nn.gelu(gp)).astype(BF16)
        y = _dot(perm, yp).astype(y_ref.dtype)
        for b in range(bsz):
            y_ref[b, t0:t0 + LRU_TB, :] = y[b * LRU_TB:(b + 1) * LRU_TB]
    hstate[...] = h
    hout_ref[...] = h
    tout_ref[...] = ubuf[0:tail]


def _lru(u, g, h_in, tail_in, w, tt, first):
    bsz, s, _ = u.shape
    rows = LRU_TB * bsz
    tail = (CONV_W - 1) * bsz
    tok = lambda i: (0, i, 0)
    const2 = lambda i: (0, 0)
    const3 = lambda i: (0, 0, 0)
    ngrp = LRU_WIDTH // GATE_GROUP
    in_specs = [
        pl.BlockSpec((bsz, tt, LRU_WIDTH), tok),
        pl.BlockSpec((bsz, tt, LRU_WIDTH), tok),
        pl.BlockSpec((bsz, LRU_WIDTH), const2),
        pl.BlockSpec((tail, LRU_WIDTH), const2),
        pl.BlockSpec((rows, rows), const2),
        pl.BlockSpec((CONV_W, LRU_WIDTH), const2),
        pl.BlockSpec((1, LRU_WIDTH), const2),
        pl.BlockSpec((ngrp, GATE_GROUP, GATE_GROUP), const3),
        pl.BlockSpec((1, LRU_WIDTH), const2),
        pl.BlockSpec((ngrp, GATE_GROUP, GATE_GROUP), const3),
        pl.BlockSpec((1, LRU_WIDTH), const2),
        pl.BlockSpec((1, LRU_WIDTH), const2),
    ]
    out_shape = [
        jax.ShapeDtypeStruct((bsz, s, LRU_WIDTH), BF16),
        jax.ShapeDtypeStruct((bsz, LRU_WIDTH), F32),
        jax.ShapeDtypeStruct((tail, LRU_WIDTH), F32),
    ]
    out_specs = [
        pl.BlockSpec((bsz, tt, LRU_WIDTH), tok),
        pl.BlockSpec((bsz, LRU_WIDTH), const2),
        pl.BlockSpec((tail, LRU_WIDTH), const2),
    ]
    assert LRU_TB == bsz
    eye = jnp.eye(rows, dtype=BF16).reshape(bsz, LRU_TB, rows)
    perm = eye.transpose(1, 0, 2).reshape(rows, rows)
    return pl.pallas_call(
        functools.partial(_lru_kernel, first),
        grid=(s // tt,), in_specs=in_specs, out_specs=out_specs, out_shape=out_shape,
        scratch_shapes=[pltpu.VMEM((tail + rows, LRU_WIDTH), F32), pltpu.VMEM((bsz, LRU_WIDTH), F32)],
        compiler_params=pltpu.CompilerParams(
            dimension_semantics=("arbitrary",), vmem_limit_bytes=VMEM_LIMIT_SMALL),
        name="lru_meta" if first else "lru",
    )(u, g, h_in, tail_in, perm, w["cw"], w["cb"], w["wa"], w["ba"], w["wx"], w["bx"], w["lam"])


def _back_kernel(h_ref, ym_ref, yl_ref, nao_ref, nlo_ref, wout_ref, n2_ref, wg_ref, wu_ref, wd_ref,
                 nf_ref, o_ref, a_scr):
    ym = _rms(ym_ref[0].astype(F32), nao_ref[...]).astype(BF16)
    yl = _rms(yl_ref[0].astype(F32), nlo_ref[...]).astype(BF16)
    half = HEADS * D_V
    h = h_ref[0] + _dot(ym, wout_ref[:half, :]) + _dot(yl, wout_ref[half:, :])
    h = _ffn_half(h, n2_ref, wg_ref, wu_ref, wd_ref, a_scr)
    o_ref[0] = _rms(h, nf_ref[...])


def _back(h1, y_mla, y_lru, w, tm):
    bsz, s, _ = h1.shape
    grid = (bsz, s // tm)
    tok = lambda b, i: (b, i, 0)
    in_specs = [
        pl.BlockSpec((1, tm, D_MODEL), tok),
        pl.BlockSpec((1, tm, HEADS * D_V), tok),
        pl.BlockSpec((1, tm, LRU_WIDTH), tok),
        _resident((1, HEADS * D_V)), _resident((1, LRU_WIDTH)),
        _resident((HEADS * D_V + LRU_WIDTH, D_MODEL)),
        _resident((1, D_MODEL)),
        _resident((D_MODEL, D_FF)), _resident((D_MODEL, D_FF)), _resident((D_FF, D_MODEL)),
        _resident((1, D_MODEL)),
    ]
    return pl.pallas_call(
        _back_kernel,
        grid=grid, in_specs=in_specs,
        out_specs=pl.BlockSpec((1, tm, D_MODEL), tok),
        out_shape=jax.ShapeDtypeStruct((bsz, s, D_MODEL), F32),
        scratch_shapes=[pltpu.VMEM((tm, D_FF), BF16)],
        compiler_params=pltpu.CompilerParams(
            dimension_semantics=("parallel", "parallel"), vmem_limit_bytes=VMEM_LIMIT_BIG),
        name="back",
    )(h1, y_mla, y_lru, w["nao"], w["nlo"], w["wout"], w["n2"], w["wg2"], w["wu2"], w["wd2"], w["nf"])


def _rope_table(pos):
    half = D_ROPE // 2
    inv_freq = ROPE_THETA ** (-jnp.arange(0, half, dtype=F32) / half)
    ang = pos.astype(F32)[:, None] * inv_freq[None, :]
    cos, sin = jnp.cos(ang), jnp.sin(ang)
    zeros = jnp.zeros_like(cos)
    pad = jnp.zeros((pos.shape[0], LANES - D_ROPE), F32)
    return jnp.stack([
        jnp.concatenate([cos, cos, pad], axis=1),
        jnp.concatenate([-sin, zeros, pad], axis=1),
        jnp.concatenate([zeros, sin, pad], axis=1),
    ])


def _block_diag_groups(wgt):
    per = GATE_GROUP // LRU_BLOCK
    w4 = wgt.reshape(LRU_BLOCKS // per, per, LRU_BLOCK, LRU_BLOCK)
    eye = jnp.eye(per, dtype=wgt.dtype)
    return jnp.einsum('mnij,nk->mnikj', w4, eye).reshape(LRU_BLOCKS // per, GATE_GROUP, GATE_GROUP)


def kernel(x, meta_tokens, ffn1_norm, ffn1_w_gate, ffn1_w_up, ffn1_w_down, mix_norm, w_in, q_latent_norm, w_uq, kv_latent_norm, w_uk, w_uv, q_head_norm, k_head_norm, conv_w, conv_b, gate_a_w, gate_a_b, gate_x_w, gate_x_b, lru_lambda, attn_out_norm, lru_out_norm, w_out, ffn2_norm, ffn2_w_gate, ffn2_w_up, ffn2_w_down, final_norm):
    l = 0
    row = lambda a: a[l].reshape(1, -1).astype(F32)
    bf = lambda a: a[l].astype(BF16)

    o2 = Q_RANK + KV_RANK
    o3 = o2 + D_ROPE
    win = w_in[l]
    win_p = jnp.concatenate(
        [win[:, :o3], jnp.zeros((D_MODEL, LANES - D_ROPE), win.dtype), win[:, o3:]], axis=1)
    wuq_p = jnp.pad(w_uq[l].reshape(Q_RANK, HEADS, D_QK),
                    ((0, 0), (0, 0), (0, D_HEAD_PAD - D_QK))).reshape(Q_RANK, HEADS * D_HEAD_PAD)
    pad_head = lambda a: jnp.pad(a[l].astype(F32), (0, D_HEAD_PAD - D_QK)).reshape(1, D_HEAD_PAD)
    w = {
        "n1": row(ffn1_norm), "wg1": bf(ffn1_w_gate), "wu1": bf(ffn1_w_up), "wd1": bf(ffn1_w_down),
        "nmix": row(mix_norm), "win": win_p.astype(BF16),
        "nq": row(q_latent_norm), "wuq": wuq_p.astype(BF16),
        "nkv": row(kv_latent_norm), "wuk": bf(w_uk), "wuv": bf(w_uv),
        "qhn": pad_head(q_head_norm), "khn": pad_head(k_head_norm),
        "cw": conv_w[l].astype(F32), "cb": row(conv_b),
        "wa": _block_diag_groups(gate_a_w[l]).astype(BF16), "ba": row(gate_a_b),
        "wx": _block_diag_groups(gate_x_w[l]).astype(BF16), "bx": row(gate_x_b),
        "lam": row(lru_lambda),
        "nao": row(attn_out_norm), "nlo": row(lru_out_norm), "wout": bf(w_out),
        "n2": row(ffn2_norm), "wg2": bf(ffn2_w_gate), "wu2": bf(ffn2_w_up), "wd2": bf(ffn2_w_down),
        "nf": row(final_norm),
    }

    meta = meta_tokens.astype(x.dtype)[None]
    rope_meta = _rope_table(jnp.arange(N_META))
    _, _, k_meta, v_meta, u_meta, g_meta = _front(meta, rope_meta, w, N_META)
    bsz = x.shape[0]
    rep = lambda a: jnp.broadcast_to(a, (bsz,) + a.shape[1:])
    zero_h = jnp.zeros((bsz, LRU_WIDTH), F32)
    zero_tail = jnp.zeros(((CONV_W - 1) * bsz, LRU_WIDTH), F32)
    _, h_meta, tail_meta = _lru(rep(u_meta), rep(g_meta), zero_h, zero_tail, w, N_META, True)

    rope_main = _rope_table(N_META + jnp.arange(SEQ))
    h1, q, k, v, u, g = _front(x, rope_main, w, TM)
    y_mla = _attention(q, k, v, k_meta, v_meta)
    y_lru, _, _ = _lru(u, g, h_meta, tail_meta, w, LRU_TT, False)
    return _back(h1, y_mla, y_lru, w, TM)
```

```python
import functools
import math

import jax
import jax.numpy as jnp
from jax import lax
from jax.experimental import pallas as pl
from jax.experimental.pallas import tpu as pltpu

D_MODEL = 1024
SEQ = 2048
N_META = 16
CHUNK = 64
HEADS = 4
D_NOPE = 128
D_ROPE = 64
D_QK = D_NOPE + D_ROPE
D_V = 128
KV_RANK = 256
Q_RANK = 384
ROPE_THETA = 10000.0
LRU_WIDTH = 512
LRU_BLOCKS = 8
LRU_BLOCK = 64
CONV_W = 4
C_RGLRU = 8.0
D_FF = 2816
EPS = 1e-6
NEG_INF = -1e30

LANES = 128
D_HEAD_PAD = 2 * LANES
Z_WIDTH = Q_RANK + KV_RANK + LANES + 2 * LRU_WIDTH
GATE_GROUP = 256

TM = 256
FF_CHUNK = 256
TQ = 256
ATTN_HEADS_PER_STEP = 2
LRU_TB = 16
LRU_TT = 128
VMEM_LIMIT_BIG = 56 * 1024 * 1024
VMEM_LIMIT_SMALL = 32 * 1024 * 1024

F32 = jnp.float32
BF16 = jnp.bfloat16


def _rms(x, g):
    ms = jnp.mean(x * x, axis=-1, keepdims=True)
    return x * lax.rsqrt(ms + EPS) * g


def _dot(a, b):
    return jnp.dot(a, b, preferred_element_type=F32)


def _dot_nt(a, b):
    return lax.dot_general(a, b, (((1,), (1,)), ((), ())), preferred_element_type=F32)


def _sigmoid(x):
    return 0.5 * jnp.tanh(0.5 * x) + 0.5


def _swiglu(u, wg_ref, wu_ref, wd_ref, a_ref, hooks):
    for c in range(D_FF // FF_CHUNK):
        sl = slice(c * FF_CHUNK, (c + 1) * FF_CHUNK)
        gate = _dot(u, wg_ref[:, sl])
        up = _dot(u, wu_ref[:, sl])
        a_ref[:, sl] = (jax.nn.silu(gate) * up).astype(BF16)
        if c in hooks:
            hooks[c]()
    return _dot(a_ref[...], wd_ref[...])


def _front_kernel(x_ref, rope_ref, n1_ref, wg_ref, wu_ref, wd_ref, nmix_ref, win_ref,
                  nq_ref, wuq_ref, nkv_ref, wuk_ref, wuv_ref, qhn_ref, khn_ref,
                  h_ref, q_ref, k_ref, v_ref, u_ref, g_ref, a_scr, un_scr):
    @pl.when(pl.program_id(0) == 0)
    def _():
        un_scr[...] = jnp.zeros_like(un_scr)

    st = {}

    def mix_in():
        z = _dot(un_scr[...], win_ref[...])
        o1 = Q_RANK
        o2 = o1 + KV_RANK
        o3 = o2 + LANES
        o4 = o3 + LRU_WIDTH
        st["c_q"], st["c_kv"], st["k_r"] = z[:, :o1], z[:, o1:o2], z[:, o2:o3]
        u_ref[0] = z[:, o3:o4].astype(u_ref.dtype)
        g_ref[0] = z[:, o4:].astype(g_ref.dtype)

    def mix_proj():
        st["q_all"] = _dot(_rms(st["c_q"], nq_ref[...]).astype(BF16), wuq_ref[...])
        ckv = _rms(st["c_kv"], nkv_ref[...]).astype(BF16)
        st["k_nope"] = _dot(ckv, wuk_ref[...])
        v_ref[0] = _dot(ckv, wuv_ref[...]).astype(v_ref.dtype)

    def mix_heads():
        cos, sin_lo, sin_hi = rope_ref[0], rope_ref[1], rope_ref[2]

        def rope(x):
            return (x * cos + pltpu.roll(x, LANES - D_ROPE // 2, 1) * sin_lo
                    + pltpu.roll(x, D_ROPE // 2, 1) * sin_hi)

        qhn = qhn_ref[...]
        khn = khn_ref[...]
        k_r = st["k_r"]
        kr_roped = rope(k_r * khn[:, LANES:])
        kr_ss = jnp.sum(k_r * k_r, axis=-1, keepdims=True)
        scale = math.log2(math.e) / math.sqrt(D_QK)
        for hd in range(HEADS):
            qh = st["q_all"][:, hd * D_HEAD_PAD:(hd + 1) * D_HEAD_PAD]
            q_rinv = lax.rsqrt(jnp.sum(qh * qh, axis=-1, keepdims=True) / D_QK + EPS)
            qn = qh * q_rinv * qhn
            q_out = jnp.concatenate([qn[:, :LANES], rope(qn[:, LANES:])], axis=1) * scale
            q_ref[0, hd] = q_out.astype(q_ref.dtype)
            kn = st["k_nope"][:, hd * D_NOPE:(hd + 1) * D_NOPE]
            k_rinv = lax.rsqrt((jnp.sum(kn * kn, axis=-1, keepdims=True) + kr_ss) / D_QK + EPS)
            k_out = jnp.concatenate([kn * k_rinv * khn[:, :LANES], kr_roped * k_rinv], axis=1)
            k_ref[0, hd] = k_out.astype(k_ref.dtype)

    mix_in()
    x = x_ref[0]
    u = _rms(x, n1_ref[...]).astype(BF16)
    h = x + 0.5 * _swiglu(u, wg_ref, wu_ref, wd_ref, a_scr, {3: mix_proj, 6: mix_heads})
    h_ref[0] = h
    un_scr[...] = _rms(h, nmix_ref[...]).astype(BF16)


def _resident(shape):
    nd = len(shape)
    return pl.BlockSpec(shape, lambda *_: (0,) * nd, pipeline_mode=pl.Buffered(1))


def _front(x, rope_tab, w, tm):
    bsz, s, _ = x.shape
    ns = s // tm
    nt = bsz * ns
    cur = lambda n: jnp.minimum(n, nt - 1)
    prv = lambda n: jnp.maximum(n - 1, 0)
    tok_cur = lambda n: (cur(n) // ns, cur(n) % ns, 0)
    tok_prv = lambda n: (prv(n) // ns, prv(n) % ns, 0)
    head_prv = lambda n: (prv(n) // ns, 0, prv(n) % ns, 0)
    in_specs = [
        pl.BlockSpec((1, tm, D_MODEL), tok_cur),
        pl.BlockSpec((3, tm, LANES), lambda n: (0, prv(n) % ns, 0)),
        _resident((1, D_MODEL)),
        _resident((D_MODEL, D_FF)), _resident((D_MODEL, D_FF)), _resident((D_FF, D_MODEL)),
        _resident((1, D_MODEL)), _resident((D_MODEL, Z_WIDTH)),
        _resident((1, Q_RANK)), _resident((Q_RANK, HEADS * D_HEAD_PAD)),
        _resident((1, KV_RANK)), _resident((KV_RANK, HEADS * D_NOPE)), _resident((KV_RANK, HEADS * D_V)),
        _resident((1, D_HEAD_PAD)), _resident((1, D_HEAD_PAD)),
    ]
    out_shape = [
        jax.ShapeDtypeStruct((bsz, s, D_MODEL), F32),
        jax.ShapeDtypeStruct((bsz, HEADS, s, D_HEAD_PAD), BF16),
        jax.ShapeDtypeStruct((bsz, HEADS, s, D_HEAD_PAD), BF16),
        jax.ShapeDtypeStruct((bsz, s, HEADS * D_V), BF16),
        jax.ShapeDtypeStruct((bsz, s, LRU_WIDTH), BF16),
        jax.ShapeDtypeStruct((bsz, s, LRU_WIDTH), BF16),
    ]
    out_specs = [
        pl.BlockSpec((1, tm, D_MODEL), tok_cur),
        pl.BlockSpec((1, HEADS, tm, D_HEAD_PAD), head_prv),
        pl.BlockSpec((1, HEADS, tm, D_HEAD_PAD), head_prv),
        pl.BlockSpec((1, tm, HEADS * D_V), tok_prv),
        pl.BlockSpec((1, tm, LRU_WIDTH), tok_prv),
        pl.BlockSpec((1, tm, LRU_WIDTH), tok_prv),
    ]
    return pl.pallas_call(
        _front_kernel,
        grid=(nt + 1,), in_specs=in_specs, out_specs=out_specs, out_shape=out_shape,
        scratch_shapes=[pltpu.VMEM((tm, D_FF), BF16), pltpu.VMEM((tm, D_MODEL), BF16)],
        compiler_params=pltpu.CompilerParams(
            dimension_semantics=("arbitrary",), vmem_limit_bytes=VMEM_LIMIT_BIG),
        name="front",
    )(x, rope_tab, w["n1"], w["wg1"], w["wu1"], w["wd1"], w["nmix"], w["win"],
      w["nq"], w["wuq"], w["nkv"], w["wuk"], w["wuv"], w["qhn"], w["khn"])


def _attn_kernel(q_ref, k_ref, v_ref, km_ref, vm_ref, o_ref, vx_scr, vmx_scr):
    hp, seq = q_ref.shape[1], q_ref.shape[2]
    for hd in range(hp):
        vcol = slice(hd * D_V, (hd + 1) * D_V)
        vx_scr[hd, :, :D_V] = v_ref[0, :, vcol]
        vx_scr[hd, :, D_V:] = jnp.ones((seq, D_V), BF16)
        vmx_scr[hd, :, :D_V] = vm_ref[0, :, vcol]
        vmx_scr[hd, :, D_V:] = jnp.ones((N_META, D_V), BF16)
    row = lax.broadcasted_iota(jnp.int32, (TQ, TQ), 0) // CHUNK
    col = lax.broadcasted_iota(jnp.int32, (TQ, TQ), 1) // CHUNK
    diag_mask = col <= row

    def scores(i, hd):
        lo, hi = i * TQ, (i + 1) * TQ
        q = q_ref[0, hd, lo:hi, :]
        s_meta = _dot_nt(q, km_ref[0, hd])
        s_diag = jnp.where(diag_mask, _dot_nt(q, k_ref[0, hd, lo:hi, :]), NEG_INF)
        s_past = _dot_nt(q, k_ref[0, hd, :lo, :]) if i > 0 else None
        return s_meta, s_diag, s_past

    def finish(i, hd, s_meta, s_diag, s_past):
        lo, hi = i * TQ, (i + 1) * TQ
        m = jnp.maximum(jnp.max(s_meta, axis=-1, keepdims=True),
                        jnp.max(s_diag, axis=-1, keepdims=True))
        if s_past is not None:
            m = jnp.maximum(m, jnp.max(s_past, axis=-1, keepdims=True))
        acc = (_dot(jnp.exp2(s_meta - m).astype(BF16), vmx_scr[hd])
               + _dot(jnp.exp2(s_diag - m).astype(BF16), vx_scr[hd, lo:hi, :]))
        if s_past is not None:
            acc = acc + _dot(jnp.exp2(s_past - m).astype(BF16), vx_scr[hd, :lo, :])
        o_ref[0, lo:hi, hd * D_V:(hd + 1) * D_V] = (acc[:, :D_V] / acc[:, D_V:]).astype(o_ref.dtype)

    items = [(i, hd) for i in range(seq // TQ) for hd in range(hp)]
    s_next = scores(*items[0])
    for n, item in enumerate(items):
        s_cur = s_next
        s_next = scores(*items[n + 1]) if n + 1 < len(items) else None
        finish(*item, *s_cur)


def _attention(q, k, v, k_meta, v_meta):
    bsz, _, s, _ = q.shape
    hp = ATTN_HEADS_PER_STEP
    grid = (bsz, HEADS // hp)
    in_specs = [
        pl.BlockSpec((1, hp, s, D_HEAD_PAD), lambda b, h: (b, h, 0, 0)),
        pl.BlockSpec((1, hp, s, D_HEAD_PAD), lambda b, h: (b, h, 0, 0)),
        pl.BlockSpec((1, s, hp * D_V), lambda b, h: (b, 0, h)),
        pl.BlockSpec((1, hp, N_META, D_HEAD_PAD), lambda b, h: (0, h, 0, 0)),
        pl.BlockSpec((1, N_META, hp * D_V), lambda b, h: (0, 0, h)),
    ]
    return pl.pallas_call(
        _attn_kernel,
        grid=grid, in_specs=in_specs,
        out_specs=pl.BlockSpec((1, s, hp * D_V), lambda b, h: (b, 0, h)),
        out_shape=jax.ShapeDtypeStruct((bsz, s, HEADS * D_V), BF16),
        scratch_shapes=[pltpu.VMEM((hp, s, 2 * D_V), BF16), pltpu.VMEM((hp, N_META, 2 * D_V), BF16)],
        compiler_params=pltpu.CompilerParams(
            dimension_semantics=("parallel", "parallel"),
            vmem_limit_bytes=VMEM_LIMIT_SMALL),
        name="attn",
    )(q, k, v, k_meta, v_meta)


def _lru_kernel(first, u_ref, g_ref, hin_ref, tin_ref, perm_ref, cw_ref, cb_ref, wa_ref, ba_ref,
                wx_ref, bx_ref, lam_ref, y_ref, hout_ref, tout_ref, ubuf, hstate):
    j = pl.program_id(0)
    bsz, tt, _ = u_ref.shape
    rows = LRU_TB * bsz
    tail = (CONV_W - 1) * bsz

    @pl.when(j == 0)
    def _():
        ubuf[0:tail] = tin_ref[...]
        hstate[...] = hin_ref[...]

    perm = perm_ref[...]
    neg_lam = -lam_ref[...]
    softplus = jnp.maximum(neg_lam, 0.0) + jnp.log1p(jnp.exp(-jnp.abs(neg_lam)))
    decay = -C_RGLRU * softplus
    h = hstate[...]
    for k in range(tt // LRU_TB):
        t0 = k * LRU_TB
        uk = jnp.concatenate([u_ref[b, t0:t0 + LRU_TB, :] for b in range(bsz)], axis=0)
        gk = jnp.concatenate([g_ref[b, t0:t0 + LRU_TB, :] for b in range(bsz)], axis=0)
        ubuf[tail:tail + rows] = _dot(perm, uk)
        gp = _dot(perm, gk)
        xc = cb_ref[...]
        for kk in range(CONV_W):
            xc = xc + cw_ref[kk:kk + 1, :] * ubuf[kk * bsz:kk * bsz + rows, :]
        ubuf[0:tail] = ubuf[rows:rows + tail]
        xcb = xc.astype(BF16)

        def gate(w_ref, b_ref):
            parts = [_dot(xcb[:, m * GATE_GROUP:(m + 1) * GATE_GROUP], w_ref[m])
                     for m in range(LRU_WIDTH // GATE_GROUP)]
            return _sigmoid(jnp.concatenate(parts, axis=1) + b_ref[...])

        log_a = decay * gate(wa_ref, ba_ref)
        a = jnp.exp(log_a)
        mult = jnp.sqrt(-jnp.tanh(log_a) * (a * a + 1.0))
        if first and k == 0:
            row = lax.broadcasted_iota(jnp.int32, (rows, LRU_WIDTH), 0)
            mult = jnp.where((row < bsz) & (j == 0), 1.0, mult)
        bt = mult * (gate(wx_ref, bx_ref) * xc)
        hs = []
        for t in range(LRU_TB):
            sl = slice(t * bsz, (t + 1) * bsz)
            h = a[sl] * h + bt[sl]
            hs.append(h)
        yp = (jnp.concatenate(hs, axis=0) * jax.nn.gelu(gp)).astype(BF16)
        y = _dot(perm, yp).astype(y_ref.dtype)
        for b in range(bsz):
            y_ref[b, t0:t0 + LRU_TB, :] = y[b * LRU_TB:(b + 1) * LRU_TB]
    hstate[...] = h
    hout_ref[...] = h
    tout_ref[...] = ubuf[0:tail]


def _lru(u, g, h_in, tail_in, w, tt, first):
    bsz, s, _ = u.shape
    rows = LRU_TB * bsz
    tail = (CONV_W - 1) * bsz
    tok = lambda i: (0, i, 0)
    const2 = lambda i: (0, 0)
    const3 = lambda i: (0, 0, 0)
    ngrp = LRU_WIDTH // GATE_GROUP
    in_specs = [
        pl.BlockSpec((bsz, tt, LRU_WIDTH), tok),
        pl.BlockSpec((bsz, tt, LRU_WIDTH), tok),
        pl.BlockSpec((bsz, LRU_WIDTH), const2),
        pl.BlockSpec((tail, LRU_WIDTH), const2),
        pl.BlockSpec((rows, rows), const2),
        pl.BlockSpec((CONV_W, LRU_WIDTH), const2),
        pl.BlockSpec((1, LRU_WIDTH), const2),
        pl.BlockSpec((ngrp, GATE_GROUP, GATE_GROUP), const3),
        pl.BlockSpec((1, LRU_WIDTH), const2),
        pl.BlockSpec((ngrp, GATE_GROUP, GATE_GROUP), const3),
        pl.BlockSpec((1, LRU_WIDTH), const2),
        pl.BlockSpec((1, LRU_WIDTH), const2),
    ]
    out_shape = [
        jax.ShapeDtypeStruct((bsz, s, LRU_WIDTH), BF16),
        jax.ShapeDtypeStruct((bsz, LRU_WIDTH), F32),
        jax.ShapeDtypeStruct((tail, LRU_WIDTH), F32),
    ]
    out_specs = [
        pl.BlockSpec((bsz, tt, LRU_WIDTH), tok),
        pl.BlockSpec((bsz, LRU_WIDTH), const2),
        pl.BlockSpec((tail, LRU_WIDTH), const2),
    ]
    assert LRU_TB == bsz
    eye = jnp.eye(rows, dtype=BF16).reshape(bsz, LRU_TB, rows)
    perm = eye.transpose(1, 0, 2).reshape(rows, rows)
    return pl.pallas_call(
        functools.partial(_lru_kernel, first),
        grid=(s // tt,), in_specs=in_specs, out_specs=out_specs, out_shape=out_shape,
        scratch_shapes=[pltpu.VMEM((tail + rows, LRU_WIDTH), F32), pltpu.VMEM((bsz, LRU_WIDTH), F32)],
        compiler_params=pltpu.CompilerParams(
            dimension_semantics=("arbitrary",), vmem_limit_bytes=VMEM_LIMIT_SMALL),
        name="lru_meta" if first else "lru",
    )(u, g, h_in, tail_in, perm, w["cw"], w["cb"], w["wa"], w["ba"], w["wx"], w["bx"], w["lam"])


def _back_kernel(h_ref, ym_ref, yl_ref, nao_ref, nlo_ref, wout_ref, n2_ref, wg_ref, wu_ref, wd_ref,
                 nf_ref, o_ref, a_scr, u2_scr, h2_scr):
    @pl.when(pl.program_id(0) == 0)
    def _():
        u2_scr[...] = jnp.zeros_like(u2_scr)
        h2_scr[...] = jnp.zeros_like(h2_scr)

    st = {}

    def mix_out():
        ym = _rms(ym_ref[0].astype(F32), nao_ref[...]).astype(BF16)
        yl = _rms(yl_ref[0].astype(F32), nlo_ref[...]).astype(BF16)
        half = HEADS * D_V
        h2 = h_ref[0] + _dot(ym, wout_ref[:half, :]) + _dot(yl, wout_ref[half:, :])
        st["h2"] = h2
        st["u2"] = _rms(h2, n2_ref[...]).astype(BF16)

    ffn = _swiglu(u2_scr[...], wg_ref, wu_ref, wd_ref, a_scr, {4: mix_out})
    o_ref[0] = _rms(h2_scr[...] + 0.5 * ffn, nf_ref[...])
    u2_scr[...] = st["u2"]
    h2_scr[...] = st["h2"]


def _back(h1, y_mla, y_lru, w, tm):
    bsz, s, _ = h1.shape
    ns = s // tm
    nt = bsz * ns
    cur = lambda n: jnp.minimum(n, nt - 1)
    prv = lambda n: jnp.maximum(n - 1, 0)
    tok_cur = lambda n: (cur(n) // ns, cur(n) % ns, 0)
    tok_prv = lambda n: (prv(n) // ns, prv(n) % ns, 0)
    in_specs = [
        pl.BlockSpec((1, tm, D_MODEL), tok_cur),
        pl.BlockSpec((1, tm, HEADS * D_V), tok_cur),
        pl.BlockSpec((1, tm, LRU_WIDTH), tok_cur),
        _resident((1, HEADS * D_V)), _resident((1, LRU_WIDTH)),
        _resident((HEADS * D_V + LRU_WIDTH, D_MODEL)),
        _resident((1, D_MODEL)),
        _resident((D_MODEL, D_FF)), _resident((D_MODEL, D_FF)), _resident((D_FF, D_MODEL)),
        _resident((1, D_MODEL)),
    ]
    return pl.pallas_call(
        _back_kernel,
        grid=(nt + 1,), in_specs=in_specs,
        out_specs=pl.BlockSpec((1, tm, D_MODEL), tok_prv),
        out_shape=jax.ShapeDtypeStruct((bsz, s, D_MODEL), F32),
        scratch_shapes=[pltpu.VMEM((tm, D_FF), BF16), pltpu.VMEM((tm, D_MODEL), BF16),
                        pltpu.VMEM((tm, D_MODEL), F32)],
        compiler_params=pltpu.CompilerParams(
            dimension_semantics=("arbitrary",), vmem_limit_bytes=VMEM_LIMIT_BIG),
        name="back",
    )(h1, y_mla, y_lru, w["nao"], w["nlo"], w["wout"], w["n2"], w["wg2"], w["wu2"], w["wd2"], w["nf"])


def _rope_table(pos):
    half = D_ROPE // 2
    inv_freq = ROPE_THETA ** (-jnp.arange(0, half, dtype=F32) / half)
    ang = pos.astype(F32)[:, None] * inv_freq[None, :]
    cos, sin = jnp.cos(ang), jnp.sin(ang)
    zeros = jnp.zeros_like(cos)
    pad = jnp.zeros((pos.shape[0], LANES - D_ROPE), F32)
    return jnp.stack([
        jnp.concatenate([cos, cos, pad], axis=1),
        jnp.concatenate([-sin, zeros, pad], axis=1),
        jnp.concatenate([zeros, sin, pad], axis=1),
    ])


def _block_diag_groups(wgt):
    per = GATE_GROUP // LRU_BLOCK
    w4 = wgt.reshape(LRU_BLOCKS // per, per, LRU_BLOCK, LRU_BLOCK)
    eye = jnp.eye(per, dtype=wgt.dtype)
    return jnp.einsum('mnij,nk->mnikj', w4, eye).reshape(LRU_BLOCKS // per, GATE_GROUP, GATE_GROUP)


def kernel(x, meta_tokens, ffn1_norm, ffn1_w_gate, ffn1_w_up, ffn1_w_down, mix_norm, w_in, q_latent_norm, w_uq, kv_latent_norm, w_uk, w_uv, q_head_norm, k_head_norm, conv_w, conv_b, gate_a_w, gate_a_b, gate_x_w, gate_x_b, lru_lambda, attn_out_norm, lru_out_norm, w_out, ffn2_norm, ffn2_w_gate, ffn2_w_up, ffn2_w_down, final_norm):
    l = 0
    row = lambda a: a[l].reshape(1, -1).astype(F32)
    bf = lambda a: a[l].astype(BF16)

    o2 = Q_RANK + KV_RANK
    o3 = o2 + D_ROPE
    win = w_in[l]
    win_p = jnp.concatenate(
        [win[:, :o3], jnp.zeros((D_MODEL, LANES - D_ROPE), win.dtype), win[:, o3:]], axis=1)
    wuq_p = jnp.pad(w_uq[l].reshape(Q_RANK, HEADS, D_QK),
                    ((0, 0), (0, 0), (0, D_HEAD_PAD - D_QK))).reshape(Q_RANK, HEADS * D_HEAD_PAD)
    pad_head = lambda a: jnp.pad(a[l].astype(F32), (0, D_HEAD_PAD - D_QK)).reshape(1, D_HEAD_PAD)
    w = {
        "n1": row(ffn1_norm), "wg1": bf(ffn1_w_gate), "wu1": bf(ffn1_w_up), "wd1": bf(ffn1_w_down),
        "nmix": row(mix_norm), "win": win_p.astype(BF16),
        "nq": row(q_latent_norm), "wuq": wuq_p.astype(BF16),
        "nkv": row(kv_latent_norm), "wuk": bf(w_uk), "wuv": bf(w_uv),
        "qhn": pad_head(q_head_norm), "khn": pad_head(k_head_norm),
        "cw": conv_w[l].astype(F32), "cb": row(conv_b),
        "wa": _block_diag_groups(gate_a_w[l]).astype(BF16), "ba": row(gate_a_b),
        "wx": _block_diag_groups(gate_x_w[l]).astype(BF16), "bx": row(gate_x_b),
        "lam": row(lru_lambda),
        "nao": row(attn_out_norm), "nlo": row(lru_out_norm), "wout": bf(w_out),
        "n2": row(ffn2_norm), "wg2": bf(ffn2_w_gate), "wu2": bf(ffn2_w_up), "wd2": bf(ffn2_w_down),
        "nf": row(final_norm),
    }

    meta = meta_tokens.astype(x.dtype)[None]
    rope_meta = _rope_table(jnp.arange(N_META))
    _, _, k_meta, v_meta, u_meta, g_meta = _front(meta, rope_meta, w, N_META)
    bsz = x.shape[0]
    rep = lambda a: jnp.broadcast_to(a, (bsz,) + a.shape[1:])
    zero_h = jnp.zeros((bsz, LRU_WIDTH), F32)
    zero_tail = jnp.zeros(((CONV_W - 1) * bsz, LRU_WIDTH), F32)
    _, h_meta, tail_meta = _lru(rep(u_meta), rep(g_meta), zero_h, zero_tail, w, N_META, True)

    rope_main = _rope_table(N_META + jnp.arange(SEQ))
    h1, q, k, v, u, g = _front(x, rope_main, w, TM)
    y_mla = _attention(q, k, v, k_meta, v_meta)
    y_lru, _, _ = _lru(u, g, h_meta, tail_meta, w, LRU_TT, False)
    return _back(h1, y_mla, y_lru, w, TM)
```

```python
import functools
import math

import jax
import jax.numpy as jnp
from jax import lax
from jax.experimental import pallas as pl
from jax.experimental.pallas import tpu as pltpu

D_MODEL = 1024
SEQ = 2048
N_META = 16
CHUNK = 64
HEADS = 4
D_NOPE = 128
D_ROPE = 64
D_QK = D_NOPE + D_ROPE
D_V = 128
KV_RANK = 256
Q_RANK = 384
ROPE_THETA = 10000.0
LRU_WIDTH = 512
LRU_BLOCKS = 8
LRU_BLOCK = 64
CONV_W = 4
C_RGLRU = 8.0
D_FF = 2816
EPS = 1e-6
NEG_INF = -1e30

LANES = 128
D_HEAD_PAD = 2 * LANES
Z_WIDTH = Q_RANK + KV_RANK + LANES + 2 * LRU_WIDTH
GATE_GROUP = 256

TM = 512
FF_CHUNK = 256
DOWN_ROWS = 256
TQ = 256
ATTN_HEADS_PER_STEP = 2
LRU_TB = 16
LRU_TT = 128
VMEM_LIMIT_BIG = 56 * 1024 * 1024
VMEM_LIMIT_SMALL = 32 * 1024 * 1024

F32 = jnp.float32
BF16 = jnp.bfloat16


def _rms(x, g):
    ms = jnp.mean(x * x, axis=-1, keepdims=True)
    return x * lax.rsqrt(ms + EPS) * g


def _dot(a, b):
    return jnp.dot(a, b, preferred_element_type=F32)


def _dot_nt(a, b):
    return lax.dot_general(a, b, (((1,), (1,)), ((), ())), preferred_element_type=F32)


def _sigmoid(x):
    return 0.5 * jnp.tanh(0.5 * x) + 0.5


def _swiglu(u, wg_ref, wu_ref, wd_ref, a_ref, hooks, finish):
    for c in range(D_FF // FF_CHUNK):
        sl = slice(c * FF_CHUNK, (c + 1) * FF_CHUNK)
        gate = _dot(u, wg_ref[:, sl])
        up = _dot(u, wu_ref[:, sl])
        a_ref[:, sl] = (jax.nn.silu(gate) * up).astype(BF16)
        if c in hooks:
            hooks[c]()
    tm = a_ref.shape[0]
    blk = min(tm, DOWN_ROWS)
    for r in range(tm // blk):
        rows = slice(r * blk, (r + 1) * blk)
        finish(rows, _dot(a_ref[rows, :], wd_ref[...]))


def _front_kernel(x_ref, rope_ref, n1_ref, wg_ref, wu_ref, wd_ref, nmix_ref, win_ref,
                  nq_ref, wuq_ref, nkv_ref, wuk_ref, wuv_ref, qhn_ref, khn_ref,
                  h_ref, q_ref, k_ref, v_ref, u_ref, g_ref, a_scr, un_scr):
    step = pl.program_id(0)
    last = pl.num_programs(0) - 1

    @pl.when(step == 0)
    def _():
        un_scr[...] = jnp.zeros_like(un_scr)

    st = {}

    def mix_in():
        z = _dot(un_scr[...], win_ref[...])
        o1 = Q_RANK
        o2 = o1 + KV_RANK
        o3 = o2 + LANES
        o4 = o3 + LRU_WIDTH
        st["c_q"], st["c_kv"], st["k_r"] = z[:, :o1], z[:, o1:o2], z[:, o2:o3]
        u_ref[0] = z[:, o3:o4].astype(u_ref.dtype)
        g_ref[0] = z[:, o4:].astype(g_ref.dtype)

    def mix_proj():
        st["q_all"] = _dot(_rms(st["c_q"], nq_ref[...]).astype(BF16), wuq_ref[...])
        ckv = _rms(st["c_kv"], nkv_ref[...]).astype(BF16)
        st["k_nope"] = _dot(ckv, wuk_ref[...])
        v_ref[0] = _dot(ckv, wuv_ref[...]).astype(v_ref.dtype)

    def mix_heads():
        cos, sin_lo, sin_hi = rope_ref[0], rope_ref[1], rope_ref[2]

        def rope(x):
            return (x * cos + pltpu.roll(x, LANES - D_ROPE // 2, 1) * sin_lo
                    + pltpu.roll(x, D_ROPE // 2, 1) * sin_hi)

        qhn = qhn_ref[...]
        khn = khn_ref[...]
        k_r = st["k_r"]
        kr_roped = rope(k_r * khn[:, LANES:])
        kr_ss = jnp.sum(k_r * k_r, axis=-1, keepdims=True)
        scale = math.log2(math.e) / math.sqrt(D_QK)
        for hd in range(HEADS):
            qh = st["q_all"][:, hd * D_HEAD_PAD:(hd + 1) * D_HEAD_PAD]
            q_rinv = lax.rsqrt(jnp.sum(qh * qh, axis=-1, keepdims=True) / D_QK + EPS)
            qn = qh * q_rinv * qhn
            q_out = jnp.concatenate([qn[:, :LANES], rope(qn[:, LANES:])], axis=1) * scale
            q_ref[0, hd] = q_out.astype(q_ref.dtype)
            kn = st["k_nope"][:, hd * D_NOPE:(hd + 1) * D_NOPE]
            k_rinv = lax.rsqrt((jnp.sum(kn * kn, axis=-1, keepdims=True) + kr_ss) / D_QK + EPS)
            k_out = jnp.concatenate([kn * k_rinv * khn[:, :LANES], kr_roped * k_rinv], axis=1)
            k_ref[0, hd] = k_out.astype(k_ref.dtype)

    def finish(rows, ffn):
        h = x_ref[0, rows, :] + 0.5 * ffn
        h_ref[0, rows, :] = h
        un_scr[rows, :] = _rms(h, nmix_ref[...]).astype(BF16)

    @pl.when(step < last)
    def _():
        mix_in()
        u = _rms(x_ref[0], n1_ref[...]).astype(BF16)
        _swiglu(u, wg_ref, wu_ref, wd_ref, a_scr, {3: mix_proj, 6: mix_heads}, finish)

    @pl.when(step == last)
    def _():
        mix_in()
        mix_proj()
        mix_heads()


def _resident(shape):
    nd = len(shape)
    return pl.BlockSpec(shape, lambda *_: (0,) * nd, pipeline_mode=pl.Buffered(1))


def _front(x, rope_tab, w, tm):
    bsz, s, _ = x.shape
    ns = s // tm
    nt = bsz * ns
    cur = lambda n: jnp.minimum(n, nt - 1)
    prv = lambda n: jnp.maximum(n - 1, 0)
    tok_cur = lambda n: (cur(n) // ns, cur(n) % ns, 0)
    tok_prv = lambda n: (prv(n) // ns, prv(n) % ns, 0)
    head_prv = lambda n: (prv(n) // ns, 0, prv(n) % ns, 0)
    in_specs = [
        pl.BlockSpec((1, tm, D_MODEL), tok_cur),
        pl.BlockSpec((3, tm, LANES), lambda n: (0, prv(n) % ns, 0)),
        _resident((1, D_MODEL)),
        _resident((D_MODEL, D_FF)), _resident((D_MODEL, D_FF)), _resident((D_FF, D_MODEL)),
        _resident((1, D_MODEL)), _resident((D_MODEL, Z_WIDTH)),
        _resident((1, Q_RANK)), _resident((Q_RANK, HEADS * D_HEAD_PAD)),
        _resident((1, KV_RANK)), _resident((KV_RANK, HEADS * D_NOPE)), _resident((KV_RANK, HEADS * D_V)),
        _resident((1, D_HEAD_PAD)), _resident((1, D_HEAD_PAD)),
    ]
    out_shape = [
        jax.ShapeDtypeStruct((bsz, s, D_MODEL), F32),
        jax.ShapeDtypeStruct((bsz, HEADS, s, D_HEAD_PAD), BF16),
        jax.ShapeDtypeStruct((bsz, HEADS, s, D_HEAD_PAD), BF16),
        jax.ShapeDtypeStruct((bsz, s, HEADS * D_V), BF16),
        jax.ShapeDtypeStruct((bsz, s, LRU_WIDTH), BF16),
        jax.ShapeDtypeStruct((bsz, s, LRU_WIDTH), BF16),
    ]
    out_specs = [
        pl.BlockSpec((1, tm, D_MODEL), tok_cur),
        pl.BlockSpec((1, HEADS, tm, D_HEAD_PAD), head_prv),
        pl.BlockSpec((1, HEADS, tm, D_HEAD_PAD), head_prv),
        pl.BlockSpec((1, tm, HEADS * D_V), tok_prv),
        pl.BlockSpec((1, tm, LRU_WIDTH), tok_prv),
        pl.BlockSpec((1, tm, LRU_WIDTH), tok_prv),
    ]
    return pl.pallas_call(
        _front_kernel,
        grid=(nt + 1,), in_specs=in_specs, out_specs=out_specs, out_shape=out_shape,
        scratch_shapes=[pltpu.VMEM((tm, D_FF), BF16), pltpu.VMEM((tm, D_MODEL), BF16)],
        compiler_params=pltpu.CompilerParams(
            dimension_semantics=("arbitrary",), vmem_limit_bytes=VMEM_LIMIT_BIG),
        name="front",
    )(x, rope_tab, w["n1"], w["wg1"], w["wu1"], w["wd1"], w["nmix"], w["win"],
      w["nq"], w["wuq"], w["nkv"], w["wuk"], w["wuv"], w["qhn"], w["khn"])


def _attn_kernel(q_ref, k_ref, v_ref, km_ref, vm_ref, o_ref, vx_scr, vmx_scr):
    hp, seq = q_ref.shape[1], q_ref.shape[2]
    for hd in range(hp):
        vcol = slice(hd * D_V, (hd + 1) * D_V)
        vx_scr[hd, :, :D_V] = v_ref[0, :, vcol]
        vx_scr[hd, :, D_V:] = jnp.ones((seq, D_V), BF16)
        vmx_scr[hd, :, :D_V] = vm_ref[0, :, vcol]
        vmx_scr[hd, :, D_V:] = jnp.ones((N_META, D_V), BF16)
    row = lax.broadcasted_iota(jnp.int32, (TQ, TQ), 0) // CHUNK
    col = lax.broadcasted_iota(jnp.int32, (TQ, TQ), 1) // CHUNK
    diag_mask = col <= row

    def scores(i, hd):
        lo, hi = i * TQ, (i + 1) * TQ
        q = q_ref[0, hd, lo:hi, :]
        s_meta = _dot_nt(q, km_ref[0, hd])
        s_diag = jnp.where(diag_mask, _dot_nt(q, k_ref[0, hd, lo:hi, :]), NEG_INF)
        s_past = _dot_nt(q, k_ref[0, hd, :lo, :]) if i > 0 else None
        return s_meta, s_diag, s_past

    def finish(i, hd, s_meta, s_diag, s_past):
        lo, hi = i * TQ, (i + 1) * TQ
        m = jnp.maximum(jnp.max(s_meta, axis=-1, keepdims=True),
                        jnp.max(s_diag, axis=-1, keepdims=True))
        if s_past is not None:
            m = jnp.maximum(m, jnp.max(s_past, axis=-1, keepdims=True))
        acc = (_dot(jnp.exp2(s_meta - m).astype(BF16), vmx_scr[hd])
               + _dot(jnp.exp2(s_diag - m).astype(BF16), vx_scr[hd, lo:hi, :]))
        if s_past is not None:
            acc = acc + _dot(jnp.exp2(s_past - m).astype(BF16), vx_scr[hd, :lo, :])
        o_ref[0, lo:hi, hd * D_V:(hd + 1) * D_V] = (acc[:, :D_V] / acc[:, D_V:]).astype(o_ref.dtype)

    items = [(i, hd) for i in range(seq // TQ) for hd in range(hp)]
    s_next = scores(*items[0])
    for n, item in enumerate(items):
        s_cur = s_next
        s_next = scores(*items[n + 1]) if n + 1 < len(items) else None
        finish(*item, *s_cur)


def _attention(q, k, v, k_meta, v_meta):
    bsz, _, s, _ = q.shape
    hp = ATTN_HEADS_PER_STEP
    grid = (bsz, HEADS // hp)
    in_specs = [
        pl.BlockSpec((1, hp, s, D_HEAD_PAD), lambda b, h: (b, h, 0, 0)),
        pl.BlockSpec((1, hp, s, D_HEAD_PAD), lambda b, h: (b, h, 0, 0)),
        pl.BlockSpec((1, s, hp * D_V), lambda b, h: (b, 0, h)),
        pl.BlockSpec((1, hp, N_META, D_HEAD_PAD), lambda b, h: (0, h, 0, 0)),
        pl.BlockSpec((1, N_META, hp * D_V), lambda b, h: (0, 0, h)),
    ]
    return pl.pallas_call(
        _attn_kernel,
        grid=grid, in_specs=in_specs,
        out_specs=pl.BlockSpec((1, s, hp * D_V), lambda b, h: (b, 0, h)),
        out_shape=jax.ShapeDtypeStruct((bsz, s, HEADS * D_V), BF16),
        scratch_shapes=[pltpu.VMEM((hp, s, 2 * D_V), BF16), pltpu.VMEM((hp, N_META, 2 * D_V), BF16)],
        compiler_params=pltpu.CompilerParams(
            dimension_semantics=("parallel", "parallel"),
            vmem_limit_bytes=VMEM_LIMIT_SMALL),
        name="attn",
    )(q, k, v, k_meta, v_meta)


def _lru_kernel(first, u_ref, g_ref, hin_ref, tin_ref, perm_ref, cw_ref, cb_ref, wa_ref, ba_ref,
                wx_ref, bx_ref, lam_ref, y_ref, hout_ref, tout_ref, ubuf, hstate):
    j = pl.program_id(0)
    bsz, tt, _ = u_ref.shape
    rows = LRU_TB * bsz
    tail = (CONV_W - 1) * bsz

    @pl.when(j == 0)
    def _():
        ubuf[0:tail] = tin_ref[...]
        hstate[...] = hin_ref[...]

    perm = perm_ref[...]
    neg_lam = -lam_ref[...]
    softplus = jnp.maximum(neg_lam, 0.0) + jnp.log1p(jnp.exp(-jnp.abs(neg_lam)))
    decay = -C_RGLRU * softplus
    h = hstate[...]
    for k in range(tt // LRU_TB):
        t0 = k * LRU_TB
        uk = jnp.concatenate([u_ref[b, t0:t0 + LRU_TB, :] for b in range(bsz)], axis=0)
        gk = jnp.concatenate([g_ref[b, t0:t0 + LRU_TB, :] for b in range(bsz)], axis=0)
        ubuf[tail:tail + rows] = _dot(perm, uk)
        gp = _dot(perm, gk)
        xc = cb_ref[...]
        for kk in range(CONV_W):
            xc = xc + cw_ref[kk:kk + 1, :] * ubuf[kk * bsz:kk * bsz + rows, :]
        ubuf[0:tail] = ubuf[rows:rows + tail]
        xcb = xc.astype(BF16)

        def gate(w_ref, b_ref):
            parts = [_dot(xcb[:, m * GATE_GROUP:(m + 1) * GATE_GROUP], w_ref[m])
                     for m in range(LRU_WIDTH // GATE_GROUP)]
            return _sigmoid(jnp.concatenate(parts, axis=1) + b_ref[...])

        log_a = decay * gate(wa_ref, ba_ref)
        a = jnp.exp(log_a)
        mult = jnp.sqrt(-jnp.tanh(log_a) * (a * a + 1.0))
        if first and k == 0:
            row = lax.broadcasted_iota(jnp.int32, (rows, LRU_WIDTH), 0)
            mult = jnp.where((row < bsz) & (j == 0), 1.0, mult)
        bt = mult * (gate(wx_ref, bx_ref) * xc)
        hs = []
        for t in range(LRU_TB):
            sl = slice(t * bsz, (t + 1) * bsz)
            h = a[sl] * h + bt[sl]
            hs.append(h)
        yp = (jnp.concatenate(hs, axis=0) * jax.nn.gelu(gp)).astype(BF16)
        y = _dot(perm, yp).astype(y_ref.dtype)
        for b in range(bsz):
            y_ref[b, t0:t0 + LRU_TB, :] = y[b * LRU_TB:(b + 1) * LRU_TB]
    hstate[...] = h
    hout_ref[...] = h
    tout_ref[...] = ubuf[0:tail]


def _lru(u, g, h_in, tail_in, w, tt, first):
    bsz, s, _ = u.shape
    rows = LRU_TB * bsz
    tail = (CONV_W - 1) * bsz
    tok = lambda i: (0, i, 0)
    const2 = lambda i: (0, 0)
    const3 = lambda i: (0, 0, 0)
    ngrp = LRU_WIDTH // GATE_GROUP
    in_specs = [
        pl.BlockSpec((bsz, tt, LRU_WIDTH), tok),
        pl.BlockSpec((bsz, tt, LRU_WIDTH), tok),
        pl.BlockSpec((bsz, LRU_WIDTH), const2),
        pl.BlockSpec((tail, LRU_WIDTH), const2),
        pl.BlockSpec((rows, rows), const2),
        pl.BlockSpec((CONV_W, LRU_WIDTH), const2),
        pl.BlockSpec((1, LRU_WIDTH), const2),
        pl.BlockSpec((ngrp, GATE_GROUP, GATE_GROUP), const3),
        pl.BlockSpec((1, LRU_WIDTH), const2),
        pl.BlockSpec((ngrp, GATE_GROUP, GATE_GROUP), const3),
        pl.BlockSpec((1, LRU_WIDTH), const2),
        pl.BlockSpec((1, LRU_WIDTH), const2),
    ]
    out_shape = [
        jax.ShapeDtypeStruct((bsz, s, LRU_WIDTH), BF16),
        jax.ShapeDtypeStruct((bsz, LRU_WIDTH), F32),
        jax.ShapeDtypeStruct((tail, LRU_WIDTH), F32),
    ]
    out_specs = [
        pl.BlockSpec((bsz, tt, LRU_WIDTH), tok),
        pl.BlockSpec((bsz, LRU_WIDTH), const2),
        pl.BlockSpec((tail, LRU_WIDTH), const2),
    ]
    assert LRU_TB == bsz
    eye = jnp.eye(rows, dtype=BF16).reshape(bsz, LRU_TB, rows)
    perm = eye.transpose(1, 0, 2).reshape(rows, rows)
    return pl.pallas_call(
        functools.partial(_lru_kernel, first),
        grid=(s // tt,), in_specs=in_specs, out_specs=out_specs, out_shape=out_shape,
        scratch_shapes=[pltpu.VMEM((tail + rows, LRU_WIDTH), F32), pltpu.VMEM((bsz, LRU_WIDTH), F32)],
        compiler_params=pltpu.CompilerParams(
            dimension_semantics=("arbitrary",), vmem_limit_bytes=VMEM_LIMIT_SMALL),
        name="lru_meta" if first else "lru",
    )(u, g, h_in, tail_in, perm, w["cw"], w["cb"], w["wa"], w["ba"], w["wx"], w["bx"], w["lam"])


def _back_kernel(h_ref, ym_ref, yl_ref, nao_ref, nlo_ref, wout_ref, n2_ref, wg_ref, wu_ref, wd_ref,
                 nf_ref, o_ref, a_scr, u2_scr, h2_scr):
    @pl.when(pl.program_id(0) == 0)
    def _():
        u2_scr[...] = jnp.zeros_like(u2_scr)
        h2_scr[...] = jnp.zeros_like(h2_scr)

    st = {}

    def mix_out():
        ym = _rms(ym_ref[0].astype(F32), nao_ref[...]).astype(BF16)
        yl = _rms(yl_ref[0].astype(F32), nlo_ref[...]).astype(BF16)
        half = HEADS * D_V
        h2 = h_ref[0] + _dot(ym, wout_ref[:half, :]) + _dot(yl, wout_ref[half:, :])
        st["h2"] = h2
        st["u2"] = _rms(h2, n2_ref[...]).astype(BF16)

    def finish(rows, ffn):
        o_ref[0, rows, :] = _rms(h2_scr[rows, :] + 0.5 * ffn, nf_ref[...])

    _swiglu(u2_scr[...], wg_ref, wu_ref, wd_ref, a_scr, {4: mix_out}, finish)
    u2_scr[...] = st["u2"]
    h2_scr[...] = st["h2"]


def _back(h1, y_mla, y_lru, w, tm):
    bsz, s, _ = h1.shape
    ns = s // tm
    nt = bsz * ns
    cur = lambda n: jnp.minimum(n, nt - 1)
    prv = lambda n: jnp.maximum(n - 1, 0)
    tok_cur = lambda n: (cur(n) // ns, cur(n) % ns, 0)
    tok_prv = lambda n: (prv(n) // ns, prv(n) % ns, 0)
    in_specs = [
        pl.BlockSpec((1, tm, D_MODEL), tok_cur),
        pl.BlockSpec((1, tm, HEADS * D_V), tok_cur),
        pl.BlockSpec((1, tm, LRU_WIDTH), tok_cur),
        _resident((1, HEADS * D_V)), _resident((1, LRU_WIDTH)),
        _resident((HEADS * D_V + LRU_WIDTH, D_MODEL)),
        _resident((1, D_MODEL)),
        _resident((D_MODEL, D_FF)), _resident((D_MODEL, D_FF)), _resident((D_FF, D_MODEL)),
        _resident((1, D_MODEL)),
    ]
    return pl.pallas_call(
        _back_kernel,
        grid=(nt + 1,), in_specs=in_specs,
        out_specs=pl.BlockSpec((1, tm, D_MODEL), tok_prv),
        out_shape=jax.ShapeDtypeStruct((bsz, s, D_MODEL), F32),
        scratch_shapes=[pltpu.VMEM((tm, D_FF), BF16), pltpu.VMEM((tm, D_MODEL), BF16),
                        pltpu.VMEM((tm, D_MODEL), F32)],
        compiler_params=pltpu.CompilerParams(
            dimension_semantics=("arbitrary",), vmem_limit_bytes=VMEM_LIMIT_BIG),
        name="back",
    )(h1, y_mla, y_lru, w["nao"], w["nlo"], w["wout"], w["n2"], w["wg2"], w["wu2"], w["wd2"], w["nf"])


def _rope_table(pos):
    half = D_ROPE // 2
    inv_freq = ROPE_THETA ** (-jnp.arange(0, half, dtype=F32) / half)
    ang = pos.astype(F32)[:, None] * inv_freq[None, :]
    cos, sin = jnp.cos(ang), jnp.sin(ang)
    zeros = jnp.zeros_like(cos)
    pad = jnp.zeros((pos.shape[0], LANES - D_ROPE), F32)
    return jnp.stack([
        jnp.concatenate([cos, cos, pad], axis=1),
        jnp.concatenate([-sin, zeros, pad], axis=1),
        jnp.concatenate([zeros, sin, pad], axis=1),
    ])


def _block_diag_groups(wgt):
    per = GATE_GROUP // LRU_BLOCK
    w4 = wgt.reshape(LRU_BLOCKS // per, per, LRU_BLOCK, LRU_BLOCK)
    eye = jnp.eye(per, dtype=wgt.dtype)
    return jnp.einsum('mnij,nk->mnikj', w4, eye).reshape(LRU_BLOCKS // per, GATE_GROUP, GATE_GROUP)


def kernel(x, meta_tokens, ffn1_norm, ffn1_w_gate, ffn1_w_up, ffn1_w_down, mix_norm, w_in, q_latent_norm, w_uq, kv_latent_norm, w_uk, w_uv, q_head_norm, k_head_norm, conv_w, conv_b, gate_a_w, gate_a_b, gate_x_w, gate_x_b, lru_lambda, attn_out_norm, lru_out_norm, w_out, ffn2_norm, ffn2_w_gate, ffn2_w_up, ffn2_w_down, final_norm):
    l = 0
    row = lambda a: a[l].reshape(1, -1).astype(F32)
    bf = lambda a: a[l].astype(BF16)

    o2 = Q_RANK + KV_RANK
    o3 = o2 + D_ROPE
    win = w_in[l]
    win_p = jnp.concatenate(
        [win[:, :o3], jnp.zeros((D_MODEL, LANES - D_ROPE), win.dtype), win[:, o3:]], axis=1)
    wuq_p = jnp.pad(w_uq[l].reshape(Q_RANK, HEADS, D_QK),
                    ((0, 0), (0, 0), (0, D_HEAD_PAD - D_QK))).reshape(Q_RANK, HEADS * D_HEAD_PAD)
    pad_head = lambda a: jnp.pad(a[l].astype(F32), (0, D_HEAD_PAD - D_QK)).reshape(1, D_HEAD_PAD)
    w = {
        "n1": row(ffn1_norm), "wg1": bf(ffn1_w_gate), "wu1": bf(ffn1_w_up), "wd1": bf(ffn1_w_down),
        "nmix": row(mix_norm), "win": win_p.astype(BF16),
        "nq": row(q_latent_norm), "wuq": wuq_p.astype(BF16),
        "nkv": row(kv_latent_norm), "wuk": bf(w_uk), "wuv": bf(w_uv),
        "qhn": pad_head(q_head_norm), "khn": pad_head(k_head_norm),
        "cw": conv_w[l].astype(F32), "cb": row(conv_b),
        "wa": _block_diag_groups(gate_a_w[l]).astype(BF16), "ba": row(gate_a_b),
        "wx": _block_diag_groups(gate_x_w[l]).astype(BF16), "bx": row(gate_x_b),
        "lam": row(lru_lambda),
        "nao": row(attn_out_norm), "nlo": row(lru_out_norm), "wout": bf(w_out),
        "n2": row(ffn2_norm), "wg2": bf(ffn2_w_gate), "wu2": bf(ffn2_w_up), "wd2": bf(ffn2_w_down),
        "nf": row(final_norm),
    }

    meta = meta_tokens.astype(x.dtype)[None]
    rope_meta = _rope_table(jnp.arange(N_META))
    _, _, k_meta, v_meta, u_meta, g_meta = _front(meta, rope_meta, w, N_META)
    bsz = x.shape[0]
    rep = lambda a: jnp.broadcast_to(a, (bsz,) + a.shape[1:])
    zero_h = jnp.zeros((bsz, LRU_WIDTH), F32)
    zero_tail = jnp.zeros(((CONV_W - 1) * bsz, LRU_WIDTH), F32)
    _, h_meta, tail_meta = _lru(rep(u_meta), rep(g_meta), zero_h, zero_tail, w, N_META, True)

    rope_main = _rope_table(N_META + jnp.arange(SEQ))
    h1, q, k, v, u, g = _front(x, rope_main, w, TM)
    y_mla = _attention(q, k, v, k_meta, v_meta)
    y_lru, _, _ = _lru(u, g, h_meta, tail_meta, w, LRU_TT, False)
    return _back(h1, y_mla, y_lru, w, TM)
```

```python
import functools
import math

import jax
import jax.numpy as jnp
from jax import lax
from jax.experimental import pallas as pl
from jax.experimental.pallas import tpu as pltpu

D_MODEL = 1024
SEQ = 2048
N_META = 16
CHUNK = 64
HEADS = 4
D_NOPE = 128
D_ROPE = 64
D_QK = D_NOPE + D_ROPE
D_V = 128
KV_RANK = 256
Q_RANK = 384
ROPE_THETA = 10000.0
LRU_WIDTH = 512
LRU_BLOCKS = 8
LRU_BLOCK = 64
CONV_W = 4
C_RGLRU = 8.0
D_FF = 2816
EPS = 1e-6
NEG_INF = -1e30

LANES = 128
D_HEAD_PAD = 2 * LANES
Z_WIDTH = Q_RANK + KV_RANK + LANES + 2 * LRU_WIDTH
GATE_GROUP = 256

TM = 512
FF_CHUNK = 256
DOWN_ROWS = 256
TQ = 256
ATTN_HEADS_PER_STEP = 2
LRU_TB = 16
VMEM_LIMIT_BIG = 56 * 1024 * 1024
VMEM_LIMIT_SMALL = 40 * 1024 * 1024

F32 = jnp.float32
BF16 = jnp.bfloat16


def _rms(x, g):
    ms = jnp.mean(x * x, axis=-1, keepdims=True)
    return x * lax.rsqrt(ms + EPS) * g


def _dot(a, b):
    return jnp.dot(a, b, preferred_element_type=F32)


def _dot_nt(a, b):
    return lax.dot_general(a, b, (((1,), (1,)), ((), ())), preferred_element_type=F32)


def _sigmoid(x):
    return 0.5 * jnp.tanh(0.5 * x) + 0.5


def _swiglu(u, wg_ref, wu_ref, wd_ref, a_ref, hooks, finish):
    for c in range(D_FF // FF_CHUNK):
        sl = slice(c * FF_CHUNK, (c + 1) * FF_CHUNK)
        gate = _dot(u, wg_ref[:, sl])
        up = _dot(u, wu_ref[:, sl])
        a_ref[:, sl] = (jax.nn.silu(gate) * up).astype(BF16)
        if c in hooks:
            hooks[c]()
    tm = a_ref.shape[0]
    blk = min(tm, DOWN_ROWS)
    for r in range(tm // blk):
        rows = slice(r * blk, (r + 1) * blk)
        finish(rows, _dot(a_ref[rows, :], wd_ref[...]))


def _front_kernel(x_ref, rope_ref, n1_ref, wg_ref, wu_ref, wd_ref, nmix_ref, win_ref,
                  nq_ref, wuq_ref, nkv_ref, wuk_ref, wuv_ref, qhn_ref, khn_ref,
                  h_ref, q_ref, k_ref, v_ref, u_ref, g_ref, a_scr, un_scr):
    step = pl.program_id(0)
    last = pl.num_programs(0) - 1

    @pl.when(step == 0)
    def _():
        un_scr[...] = jnp.zeros_like(un_scr)

    st = {}

    def mix_in():
        z = _dot(un_scr[...], win_ref[...])
        o1 = Q_RANK
        o2 = o1 + KV_RANK
        o3 = o2 + LANES
        o4 = o3 + LRU_WIDTH
        st["c_q"], st["c_kv"], st["k_r"] = z[:, :o1], z[:, o1:o2], z[:, o2:o3]
        u_ref[0] = z[:, o3:o4].astype(u_ref.dtype)
        g_ref[0] = z[:, o4:].astype(g_ref.dtype)

    def mix_proj():
        st["q_all"] = _dot(_rms(st["c_q"], nq_ref[...]).astype(BF16), wuq_ref[...])
        ckv = _rms(st["c_kv"], nkv_ref[...]).astype(BF16)
        st["k_nope"] = _dot(ckv, wuk_ref[...])
        v_ref[0] = _dot(ckv, wuv_ref[...]).astype(v_ref.dtype)

    def mix_heads():
        cos, sin_lo, sin_hi = rope_ref[0], rope_ref[1], rope_ref[2]

        def rope(x):
            return (x * cos + pltpu.roll(x, LANES - D_ROPE // 2, 1) * sin_lo
                    + pltpu.roll(x, D_ROPE // 2, 1) * sin_hi)

        qhn = qhn_ref[...]
        khn = khn_ref[...]
        k_r = st["k_r"]
        kr_roped = rope(k_r * khn[:, LANES:])
        kr_ss = jnp.sum(k_r * k_r, axis=-1, keepdims=True)
        scale = math.log2(math.e) / math.sqrt(D_QK)
        for hd in range(HEADS):
            qh = st["q_all"][:, hd * D_HEAD_PAD:(hd + 1) * D_HEAD_PAD]
            q_rinv = lax.rsqrt(jnp.sum(qh * qh, axis=-1, keepdims=True) / D_QK + EPS)
            qn = qh * q_rinv * qhn
            q_out = jnp.concatenate([qn[:, :LANES], rope(qn[:, LANES:])], axis=1) * scale
            q_ref[0, hd] = q_out.astype(q_ref.dtype)
            kn = st["k_nope"][:, hd * D_NOPE:(hd + 1) * D_NOPE]
            k_rinv = lax.rsqrt((jnp.sum(kn * kn, axis=-1, keepdims=True) + kr_ss) / D_QK + EPS)
            k_out = jnp.concatenate([kn * k_rinv * khn[:, :LANES], kr_roped * k_rinv], axis=1)
            k_ref[0, hd] = k_out.astype(k_ref.dtype)

    def finish(rows, ffn):
        h = x_ref[0, rows, :] + 0.5 * ffn
        h_ref[0, rows, :] = h
        un_scr[rows, :] = _rms(h, nmix_ref[...]).astype(BF16)

    @pl.when(step < last)
    def _():
        mix_in()
        u = _rms(x_ref[0], n1_ref[...]).astype(BF16)
        _swiglu(u, wg_ref, wu_ref, wd_ref, a_scr, {3: mix_proj, 6: mix_heads}, finish)

    @pl.when(step == last)
    def _():
        mix_in()
        mix_proj()
        mix_heads()


def _resident(shape):
    nd = len(shape)
    return pl.BlockSpec(shape, lambda *_: (0,) * nd, pipeline_mode=pl.Buffered(1))


def _front(x, rope_tab, w, tm):
    bsz, s, _ = x.shape
    ns = s // tm
    nt = bsz * ns
    cur = lambda n: jnp.minimum(n, nt - 1)
    prv = lambda n: jnp.maximum(n - 1, 0)
    tok_cur = lambda n: (cur(n) // ns, cur(n) % ns, 0)
    tok_prv = lambda n: (prv(n) // ns, prv(n) % ns, 0)
    head_prv = lambda n: (prv(n) // ns, 0, prv(n) % ns, 0)
    in_specs = [
        pl.BlockSpec((1, tm, D_MODEL), tok_cur),
        pl.BlockSpec((3, tm, LANES), lambda n: (0, prv(n) % ns, 0)),
        _resident((1, D_MODEL)),
        _resident((D_MODEL, D_FF)), _resident((D_MODEL, D_FF)), _resident((D_FF, D_MODEL)),
        _resident((1, D_MODEL)), _resident((D_MODEL, Z_WIDTH)),
        _resident((1, Q_RANK)), _resident((Q_RANK, HEADS * D_HEAD_PAD)),
        _resident((1, KV_RANK)), _resident((KV_RANK, HEADS * D_NOPE)), _resident((KV_RANK, HEADS * D_V)),
        _resident((1, D_HEAD_PAD)), _resident((1, D_HEAD_PAD)),
    ]
    out_shape = [
        jax.ShapeDtypeStruct((bsz, s, D_MODEL), F32),
        jax.ShapeDtypeStruct((bsz, HEADS, s, D_HEAD_PAD), BF16),
        jax.ShapeDtypeStruct((bsz, HEADS, s, D_HEAD_PAD), BF16),
        jax.ShapeDtypeStruct((bsz, s, HEADS * D_V), BF16),
        jax.ShapeDtypeStruct((bsz, s, LRU_WIDTH), BF16),
        jax.ShapeDtypeStruct((bsz, s, LRU_WIDTH), BF16),
    ]
    out_specs = [
        pl.BlockSpec((1, tm, D_MODEL), tok_cur),
        pl.BlockSpec((1, HEADS, tm, D_HEAD_PAD), head_prv),
        pl.BlockSpec((1, HEADS, tm, D_HEAD_PAD), head_prv),
        pl.BlockSpec((1, tm, HEADS * D_V), tok_prv),
        pl.BlockSpec((1, tm, LRU_WIDTH), tok_prv),
        pl.BlockSpec((1, tm, LRU_WIDTH), tok_prv),
    ]
    return pl.pallas_call(
        _front_kernel,
        grid=(nt + 1,), in_specs=in_specs, out_specs=out_specs, out_shape=out_shape,
        scratch_shapes=[pltpu.VMEM((tm, D_FF), BF16), pltpu.VMEM((tm, D_MODEL), BF16)],
        compiler_params=pltpu.CompilerParams(
            dimension_semantics=("arbitrary",), vmem_limit_bytes=VMEM_LIMIT_BIG),
        name="front",
    )(x, rope_tab, w["n1"], w["wg1"], w["wu1"], w["wd1"], w["nmix"], w["win"],
      w["nq"], w["wuq"], w["nkv"], w["wuk"], w["wuv"], w["qhn"], w["khn"])


def _attn_body(q_ref, k_ref, v_ref, km_ref, vm_ref, o_ref, vx_scr, vmx_scr, hooks):
    hp, seq = q_ref.shape[1], q_ref.shape[2]
    for hd in range(hp):
        vcol = slice(hd * D_V, (hd + 1) * D_V)
        vx_scr[hd, :, :D_V] = v_ref[0, :, vcol]
        vx_scr[hd, :, D_V:] = jnp.ones((seq, D_V), BF16)
        vmx_scr[hd, :, :D_V] = vm_ref[0, :, vcol]
        vmx_scr[hd, :, D_V:] = jnp.ones((N_META, D_V), BF16)
    row = lax.broadcasted_iota(jnp.int32, (TQ, TQ), 0) // CHUNK
    col = lax.broadcasted_iota(jnp.int32, (TQ, TQ), 1) // CHUNK
    diag_mask = col <= row

    def scores(i, hd):
        lo, hi = i * TQ, (i + 1) * TQ
        q = q_ref[0, hd, lo:hi, :]
        s_meta = _dot_nt(q, km_ref[0, hd])
        s_diag = jnp.where(diag_mask, _dot_nt(q, k_ref[0, hd, lo:hi, :]), NEG_INF)
        s_past = _dot_nt(q, k_ref[0, hd, :lo, :]) if i > 0 else None
        return s_meta, s_diag, s_past

    def finish(i, hd, s_meta, s_diag, s_past):
        lo, hi = i * TQ, (i + 1) * TQ
        m = jnp.maximum(jnp.max(s_meta, axis=-1, keepdims=True),
                        jnp.max(s_diag, axis=-1, keepdims=True))
        if s_past is not None:
            m = jnp.maximum(m, jnp.max(s_past, axis=-1, keepdims=True))
        acc = (_dot(jnp.exp2(s_meta - m).astype(BF16), vmx_scr[hd])
               + _dot(jnp.exp2(s_diag - m).astype(BF16), vx_scr[hd, lo:hi, :]))
        if s_past is not None:
            acc = acc + _dot(jnp.exp2(s_past - m).astype(BF16), vx_scr[hd, :lo, :])
        o_ref[0, lo:hi, hd * D_V:(hd + 1) * D_V] = (acc[:, :D_V] / acc[:, D_V:]).astype(o_ref.dtype)

    items = [(i, hd) for i in range(seq // TQ) for hd in range(hp)]
    s_next = scores(*items[0])
    for n, item in enumerate(items):
        s_cur = s_next
        s_next = scores(*items[n + 1]) if n + 1 < len(items) else None
        finish(*item, *s_cur)
        for hook in hooks.get(n, ()):
            hook()


def _lru_blocks(first, step0, u_ref, g_ref, perm_ref, cw_ref, cb_ref, wa_ref, ba_ref, wx_ref, bx_ref,
                lam_ref, y_ref, ubuf, hstate):
    bsz, tt, _ = u_ref.shape
    rows = LRU_TB * bsz
    tail = (CONV_W - 1) * bsz

    def gate(xcb, w_ref, b_ref):
        parts = [_dot(xcb[:, m * GATE_GROUP:(m + 1) * GATE_GROUP], w_ref[m])
                 for m in range(LRU_WIDTH // GATE_GROUP)]
        return _sigmoid(jnp.concatenate(parts, axis=1) + b_ref[...])

    def phases(k):
        st = {}
        t0 = k * LRU_TB

        def conv():
            perm = perm_ref[...]
            uk = jnp.concatenate([u_ref[b, t0:t0 + LRU_TB, :] for b in range(bsz)], axis=0)
            gk = jnp.concatenate([g_ref[b, t0:t0 + LRU_TB, :] for b in range(bsz)], axis=0)
            ubuf[tail:tail + rows] = _dot(perm, uk)
            st["gp"] = _dot(perm, gk)
            xc = cb_ref[...]
            for kk in range(CONV_W):
                xc = xc + cw_ref[kk:kk + 1, :] * ubuf[kk * bsz:kk * bsz + rows, :]
            ubuf[0:tail] = ubuf[rows:rows + tail]
            st["xc"] = xc
            st["xcb"] = xc.astype(BF16)

        def decay_gate():
            neg_lam = -lam_ref[...]
            softplus = jnp.maximum(neg_lam, 0.0) + jnp.log1p(jnp.exp(-jnp.abs(neg_lam)))
            log_a = -C_RGLRU * softplus * gate(st["xcb"], wa_ref, ba_ref)
            a = jnp.exp(log_a)
            mult = jnp.sqrt(-jnp.tanh(log_a) * (a * a + 1.0))
            if first and k == 0:
                row = lax.broadcasted_iota(jnp.int32, (rows, LRU_WIDTH), 0)
                mult = jnp.where((row < bsz) & step0, 1.0, mult)
            st["a"], st["mult"] = a, mult

        def recurrence():
            a = st["a"]
            bt = st["mult"] * (gate(st["xcb"], wx_ref, bx_ref) * st["xc"])
            h = hstate[...]
            hs = []
            for t in range(LRU_TB):
                sl = slice(t * bsz, (t + 1) * bsz)
                h = a[sl] * h + bt[sl]
                hs.append(h)
            hstate[...] = h
            st["h"] = jnp.concatenate(hs, axis=0)

        def output():
            yp = (st["h"] * jax.nn.gelu(st["gp"])).astype(BF16)
            y = _dot(perm_ref[...], yp).astype(y_ref.dtype)
            for b in range(bsz):
                y_ref[b, t0:t0 + LRU_TB, :] = y[b * LRU_TB:(b + 1) * LRU_TB]

        return [conv, decay_gate, recurrence, output]

    return [ph for k in range(tt // LRU_TB) for ph in phases(k)]


def _lru_kernel(first, u_ref, g_ref, hin_ref, tin_ref, perm_ref, cw_ref, cb_ref, wa_ref, ba_ref,
                wx_ref, bx_ref, lam_ref, y_ref, hout_ref, tout_ref, ubuf, hstate):
    step0 = pl.program_id(0) == 0

    @pl.when(step0)
    def _():
        ubuf[0:tin_ref.shape[0]] = tin_ref[...]
        hstate[...] = hin_ref[...]

    for blk in _lru_blocks(first, step0, u_ref, g_ref, perm_ref, cw_ref, cb_ref, wa_ref, ba_ref,
                           wx_ref, bx_ref, lam_ref, y_ref, ubuf, hstate):
        blk()
    hout_ref[...] = hstate[...]
    tout_ref[...] = ubuf[0:tout_ref.shape[0]]


def _attn_lru_kernel(q_ref, k_ref, v_ref, km_ref, vm_ref, u_ref, g_ref, hin_ref, tin_ref, perm_ref,
                     cw_ref, cb_ref, wa_ref, ba_ref, wx_ref, bx_ref, lam_ref, o_ref, y_ref,
                     vx_scr, vmx_scr, ubuf, hstate):
    step0 = (pl.program_id(0) == 0) & (pl.program_id(1) == 0)

    @pl.when(step0)
    def _():
        ubuf[0:tin_ref.shape[0]] = tin_ref[...]
        hstate[...] = hin_ref[...]

    stages = _lru_blocks(False, step0, u_ref, g_ref, perm_ref, cw_ref, cb_ref, wa_ref, ba_ref,
                         wx_ref, bx_ref, lam_ref, y_ref, ubuf, hstate)
    n_items = (q_ref.shape[2] // TQ) * q_ref.shape[1]
    assert len(stages) == n_items
    hooks = {n: [stage] for n, stage in enumerate(stages)}
    _attn_body(q_ref, k_ref, v_ref, km_ref, vm_ref, o_ref, vx_scr, vmx_scr, hooks)


def _lru_specs(bsz, tt, tok, const2, const3):
    rows = LRU_TB * bsz
    tail = (CONV_W - 1) * bsz
    ngrp = LRU_WIDTH // GATE_GROUP
    return [
        pl.BlockSpec((bsz, tt, LRU_WIDTH), tok),
        pl.BlockSpec((bsz, tt, LRU_WIDTH), tok),
        pl.BlockSpec((bsz, LRU_WIDTH), const2),
        pl.BlockSpec((tail, LRU_WIDTH), const2),
        pl.BlockSpec((rows, rows), const2),
        pl.BlockSpec((CONV_W, LRU_WIDTH), const2),
        pl.BlockSpec((1, LRU_WIDTH), const2),
        pl.BlockSpec((ngrp, GATE_GROUP, GATE_GROUP), const3),
        pl.BlockSpec((1, LRU_WIDTH), const2),
        pl.BlockSpec((ngrp, GATE_GROUP, GATE_GROUP), const3),
        pl.BlockSpec((1, LRU_WIDTH), const2),
        pl.BlockSpec((1, LRU_WIDTH), const2),
    ]


def _lru_operands(u, g, h_in, tail_in, w):
    bsz = u.shape[0]
    assert LRU_TB == bsz
    rows = LRU_TB * bsz
    eye = jnp.eye(rows, dtype=BF16).reshape(bsz, LRU_TB, rows)
    perm = eye.transpose(1, 0, 2).reshape(rows, rows)
    return (u, g, h_in, tail_in, perm, w["cw"], w["cb"], w["wa"], w["ba"], w["wx"], w["bx"], w["lam"])


def _lru_scratch(bsz):
    rows = LRU_TB * bsz
    tail = (CONV_W - 1) * bsz
    return [pltpu.VMEM((tail + rows, LRU_WIDTH), F32), pltpu.VMEM((bsz, LRU_WIDTH), F32)]


def _lru(u, g, h_in, tail_in, w, tt, first):
    bsz, s, _ = u.shape
    tail = (CONV_W - 1) * bsz
    tok = lambda i: (0, i, 0)
    const2 = lambda i: (0, 0)
    const3 = lambda i: (0, 0, 0)
    out_shape = [
        jax.ShapeDtypeStruct((bsz, s, LRU_WIDTH), BF16),
        jax.ShapeDtypeStruct((bsz, LRU_WIDTH), F32),
        jax.ShapeDtypeStruct((tail, LRU_WIDTH), F32),
    ]
    out_specs = [
        pl.BlockSpec((bsz, tt, LRU_WIDTH), tok),
        pl.BlockSpec((bsz, LRU_WIDTH), const2),
        pl.BlockSpec((tail, LRU_WIDTH), const2),
    ]
    return pl.pallas_call(
        functools.partial(_lru_kernel, first),
        grid=(s // tt,), in_specs=_lru_specs(bsz, tt, tok, const2, const3),
        out_specs=out_specs, out_shape=out_shape, scratch_shapes=_lru_scratch(bsz),
        compiler_params=pltpu.CompilerParams(
            dimension_semantics=("arbitrary",), vmem_limit_bytes=VMEM_LIMIT_SMALL),
        name="lru_meta",
    )(*_lru_operands(u, g, h_in, tail_in, w))


def _attention_lru(q, k, v, k_meta, v_meta, u, g, h_in, tail_in, w):
    bsz, _, s, _ = q.shape
    hp = ATTN_HEADS_PER_STEP
    ngroups = HEADS // hp
    tt = s // (bsz * ngroups)
    assert tt % LRU_TB == 0
    in_specs = [
        pl.BlockSpec((1, hp, s, D_HEAD_PAD), lambda b, h: (b, h, 0, 0)),
        pl.BlockSpec((1, hp, s, D_HEAD_PAD), lambda b, h: (b, h, 0, 0)),
        pl.BlockSpec((1, s, hp * D_V), lambda b, h: (b, 0, h)),
        pl.BlockSpec((1, hp, N_META, D_HEAD_PAD), lambda b, h: (0, h, 0, 0)),
        pl.BlockSpec((1, N_META, hp * D_V), lambda b, h: (0, 0, h)),
    ] + _lru_specs(bsz, tt, lambda b, h: (0, b * ngroups + h, 0), lambda b, h: (0, 0),
                   lambda b, h: (0, 0, 0))
    out_specs = [
        pl.BlockSpec((1, s, hp * D_V), lambda b, h: (b, 0, h)),
        pl.BlockSpec((bsz, tt, LRU_WIDTH), lambda b, h: (0, b * ngroups + h, 0)),
    ]
    out_shape = [
        jax.ShapeDtypeStruct((bsz, s, HEADS * D_V), BF16),
        jax.ShapeDtypeStruct((bsz, s, LRU_WIDTH), BF16),
    ]
    return pl.pallas_call(
        _attn_lru_kernel,
        grid=(bsz, ngroups), in_specs=in_specs, out_specs=out_specs, out_shape=out_shape,
        scratch_shapes=[pltpu.VMEM((hp, s, 2 * D_V), BF16), pltpu.VMEM((hp, N_META, 2 * D_V), BF16)]
        + _lru_scratch(bsz),
        compiler_params=pltpu.CompilerParams(
            dimension_semantics=("arbitrary", "arbitrary"),
            vmem_limit_bytes=VMEM_LIMIT_SMALL),
        name="attn_lru",
    )(q, k, v, k_meta, v_meta, *_lru_operands(u, g, h_in, tail_in, w))


def _back_kernel(h_ref, ym_ref, yl_ref, nao_ref, nlo_ref, wout_ref, n2_ref, wg_ref, wu_ref, wd_ref,
                 nf_ref, o_ref, a_scr, u2_scr, h2_scr):
    @pl.when(pl.program_id(0) == 0)
    def _():
        u2_scr[...] = jnp.zeros_like(u2_scr)
        h2_scr[...] = jnp.zeros_like(h2_scr)

    st = {}

    def mix_out():
        ym = _rms(ym_ref[0].astype(F32), nao_ref[...]).astype(BF16)
        yl = _rms(yl_ref[0].astype(F32), nlo_ref[...]).astype(BF16)
        half = HEADS * D_V
        h2 = h_ref[0] + _dot(ym, wout_ref[:half, :]) + _dot(yl, wout_ref[half:, :])
        st["h2"] = h2
        st["u2"] = _rms(h2, n2_ref[...]).astype(BF16)

    def finish(rows, ffn):
        o_ref[0, rows, :] = _rms(h2_scr[rows, :] + 0.5 * ffn, nf_ref[...])

    _swiglu(u2_scr[...], wg_ref, wu_ref, wd_ref, a_scr, {4: mix_out}, finish)
    u2_scr[...] = st["u2"]
    h2_scr[...] = st["h2"]


def _back(h1, y_mla, y_lru, w, tm):
    bsz, s, _ = h1.shape
    ns = s // tm
    nt = bsz * ns
    cur = lambda n: jnp.minimum(n, nt - 1)
    prv = lambda n: jnp.maximum(n - 1, 0)
    tok_cur = lambda n: (cur(n) // ns, cur(n) % ns, 0)
    tok_prv = lambda n: (prv(n) // ns, prv(n) % ns, 0)
    in_specs = [
        pl.BlockSpec((1, tm, D_MODEL), tok_cur),
        pl.BlockSpec((1, tm, HEADS * D_V), tok_cur),
        pl.BlockSpec((1, tm, LRU_WIDTH), tok_cur),
        _resident((1, HEADS * D_V)), _resident((1, LRU_WIDTH)),
        _resident((HEADS * D_V + LRU_WIDTH, D_MODEL)),
        _resident((1, D_MODEL)),
        _resident((D_MODEL, D_FF)), _resident((D_MODEL, D_FF)), _resident((D_FF, D_MODEL)),
        _resident((1, D_MODEL)),
    ]
    return pl.pallas_call(
        _back_kernel,
        grid=(nt + 1,), in_specs=in_specs,
        out_specs=pl.BlockSpec((1, tm, D_MODEL), tok_prv),
        out_shape=jax.ShapeDtypeStruct((bsz, s, D_MODEL), F32),
        scratch_shapes=[pltpu.VMEM((tm, D_FF), BF16), pltpu.VMEM((tm, D_MODEL), BF16),
                        pltpu.VMEM((tm, D_MODEL), F32)],
        compiler_params=pltpu.CompilerParams(
            dimension_semantics=("arbitrary",), vmem_limit_bytes=VMEM_LIMIT_BIG),
        name="back",
    )(h1, y_mla, y_lru, w["nao"], w["nlo"], w["wout"], w["n2"], w["wg2"], w["wu2"], w["wd2"], w["nf"])


def _rope_table(pos):
    half = D_ROPE // 2
    inv_freq = ROPE_THETA ** (-jnp.arange(0, half, dtype=F32) / half)
    ang = pos.astype(F32)[:, None] * inv_freq[None, :]
    cos, sin = jnp.cos(ang), jnp.sin(ang)
    zeros = jnp.zeros_like(cos)
    pad = jnp.zeros((pos.shape[0], LANES - D_ROPE), F32)
    return jnp.stack([
        jnp.concatenate([cos, cos, pad], axis=1),
        jnp.concatenate([-sin, zeros, pad], axis=1),
        jnp.concatenate([zeros, sin, pad], axis=1),
    ])


def _block_diag_groups(wgt):
    per = GATE_GROUP // LRU_BLOCK
    w4 = wgt.reshape(LRU_BLOCKS // per, per, LRU_BLOCK, LRU_BLOCK)
    eye = jnp.eye(per, dtype=wgt.dtype)
    return jnp.einsum('mnij,nk->mnikj', w4, eye).reshape(LRU_BLOCKS // per, GATE_GROUP, GATE_GROUP)


def kernel(x, meta_tokens, ffn1_norm, ffn1_w_gate, ffn1_w_up, ffn1_w_down, mix_norm, w_in, q_latent_norm, w_uq, kv_latent_norm, w_uk, w_uv, q_head_norm, k_head_norm, conv_w, conv_b, gate_a_w, gate_a_b, gate_x_w, gate_x_b, lru_lambda, attn_out_norm, lru_out_norm, w_out, ffn2_norm, ffn2_w_gate, ffn2_w_up, ffn2_w_down, final_norm):
    l = 0
    row = lambda a: a[l].reshape(1, -1).astype(F32)
    bf = lambda a: a[l].astype(BF16)

    o2 = Q_RANK + KV_RANK
    o3 = o2 + D_ROPE
    win = w_in[l]
    win_p = jnp.concatenate(
        [win[:, :o3], jnp.zeros((D_MODEL, LANES - D_ROPE), win.dtype), win[:, o3:]], axis=1)
    wuq_p = jnp.pad(w_uq[l].reshape(Q_RANK, HEADS, D_QK),
                    ((0, 0), (0, 0), (0, D_HEAD_PAD - D_QK))).reshape(Q_RANK, HEADS * D_HEAD_PAD)
    pad_head = lambda a: jnp.pad(a[l].astype(F32), (0, D_HEAD_PAD - D_QK)).reshape(1, D_HEAD_PAD)
    w = {
        "n1": row(ffn1_norm), "wg1": bf(ffn1_w_gate), "wu1": bf(ffn1_w_up), "wd1": bf(ffn1_w_down),
        "nmix": row(mix_norm), "win": win_p.astype(BF16),
        "nq": row(q_latent_norm), "wuq": wuq_p.astype(BF16),
        "nkv": row(kv_latent_norm), "wuk": bf(w_uk), "wuv": bf(w_uv),
        "qhn": pad_head(q_head_norm), "khn": pad_head(k_head_norm),
        "cw": conv_w[l].astype(F32), "cb": row(conv_b),
        "wa": _block_diag_groups(gate_a_w[l]).astype(BF16), "ba": row(gate_a_b),
        "wx": _block_diag_groups(gate_x_w[l]).astype(BF16), "bx": row(gate_x_b),
        "lam": row(lru_lambda),
        "nao": row(attn_out_norm), "nlo": row(lru_out_norm), "wout": bf(w_out),
        "n2": row(ffn2_norm), "wg2": bf(ffn2_w_gate), "wu2": bf(ffn2_w_up), "wd2": bf(ffn2_w_down),
        "nf": row(final_norm),
    }

    meta = meta_tokens.astype(x.dtype)[None]
    rope_meta = _rope_table(jnp.arange(N_META))
    _, _, k_meta, v_meta, u_meta, g_meta = _front(meta, rope_meta, w, N_META)
    bsz = x.shape[0]
    rep = lambda a: jnp.broadcast_to(a, (bsz,) + a.shape[1:])
    zero_h = jnp.zeros((bsz, LRU_WIDTH), F32)
    zero_tail = jnp.zeros(((CONV_W - 1) * bsz, LRU_WIDTH), F32)
    _, h_meta, tail_meta = _lru(rep(u_meta), rep(g_meta), zero_h, zero_tail, w, N_META, True)

    rope_main = _rope_table(N_META + jnp.arange(SEQ))
    h1, q, k, v, u, g = _front(x, rope_main, w, TM)
    y_mla, y_lru = _attention_lru(q, k, v, k_meta, v_meta, u, g, h_meta, tail_meta, w)
    return _back(h1, y_mla, y_lru, w, TM)
```

```python
import functools
import math

import jax
import jax.numpy as jnp
from jax import lax
from jax.experimental import pallas as pl
from jax.experimental.pallas import tpu as pltpu

D_MODEL = 1024
SEQ = 2048
N_META = 16
CHUNK = 64
HEADS = 4
D_NOPE = 128
D_ROPE = 64
D_QK = D_NOPE + D_ROPE
D_V = 128
KV_RANK = 256
Q_RANK = 384
ROPE_THETA = 10000.0
LRU_WIDTH = 512
LRU_BLOCKS = 8
LRU_BLOCK = 64
CONV_W = 4
C_RGLRU = 8.0
D_FF = 2816
EPS = 1e-6
NEG_INF = -1e30

LANES = 128
BF16_SUBLANES = 16
D_HEAD_PAD = 2 * LANES
Z_WIDTH = Q_RANK + KV_RANK + LANES + 2 * LRU_WIDTH
GATE_GROUP = 256

TM = 512
FF_CHUNK = 256
DOWN_ROWS = 256
TQ = 256
ATTN_HEADS_PER_STEP = 2
LRU_TB = 16
VMEM_LIMIT_BIG = 56 * 1024 * 1024
VMEM_LIMIT_SMALL = 40 * 1024 * 1024

F32 = jnp.float32
BF16 = jnp.bfloat16


def _rms(x, g):
    ms = jnp.mean(x * x, axis=-1, keepdims=True)
    return x * lax.rsqrt(ms + EPS) * g


def _dot(a, b):
    return jnp.dot(a, b, preferred_element_type=F32)


def _dot_nt(a, b):
    return lax.dot_general(a, b, (((1,), (1,)), ((), ())), preferred_element_type=F32)


def _swiglu(u, wg_ref, wu_ref, wd_ref, a_ref, hooks, finish):
    for c in range(D_FF // FF_CHUNK):
        sl = slice(c * FF_CHUNK, (c + 1) * FF_CHUNK)
        gate = _dot(u, wg_ref[:, sl])
        up = _dot(u, wu_ref[:, sl])
        a_ref[:, sl] = (jax.nn.silu(gate) * up).astype(BF16)
        if c in hooks:
            hooks[c]()
    tm = a_ref.shape[0]
    blk = min(tm, DOWN_ROWS)
    for r in range(tm // blk):
        rows = slice(r * blk, (r + 1) * blk)
        finish(rows, _dot(a_ref[rows, :], wd_ref[...]))


def _front_kernel(n_cast, x_ref, rope_ref, n1_ref, wg_ref, wu_ref, wd_ref, nmix_ref, win_ref,
                  nq_ref, wuq_ref, nkv_ref, wuk_ref, wuv_ref, qhn_ref, khn_ref, *rest):
    cast_in, rest = rest[:n_cast], rest[n_cast:]
    h_ref, q_ref, k_ref, v_ref, u_ref, g_ref = rest[:6]
    cast_out, (a_scr, un_scr) = rest[6:6 + n_cast], rest[6 + n_cast:]
    step = pl.program_id(0)
    last = pl.num_programs(0) - 1

    @pl.when(step == 0)
    def _():
        un_scr[...] = jnp.zeros_like(un_scr)

    st = {}

    def mix_in():
        z = _dot(un_scr[...], win_ref[...])
        o1 = Q_RANK
        o2 = o1 + KV_RANK
        o3 = o2 + LANES
        o4 = o3 + LRU_WIDTH
        st["c_q"], st["c_kv"], st["k_r"] = z[:, :o1], z[:, o1:o2], z[:, o2:o3]
        u_ref[0] = z[:, o3:o4].astype(u_ref.dtype)
        g_ref[0] = z[:, o4:].astype(g_ref.dtype)

    def mix_proj():
        st["q_all"] = _dot(_rms(st["c_q"], nq_ref[...]).astype(BF16), wuq_ref[...])
        ckv = _rms(st["c_kv"], nkv_ref[...]).astype(BF16)
        st["k_nope"] = _dot(ckv, wuk_ref[...])
        v_ref[0] = _dot(ckv, wuv_ref[...]).astype(v_ref.dtype)

    def mix_heads():
        cos, sin_lo, sin_hi = rope_ref[0], rope_ref[1], rope_ref[2]

        def rope(x):
            return (x * cos + pltpu.roll(x, LANES - D_ROPE // 2, 1) * sin_lo
                    + pltpu.roll(x, D_ROPE // 2, 1) * sin_hi)

        qhn = qhn_ref[...]
        khn = khn_ref[...]
        k_r = st["k_r"]
        kr_roped = rope(k_r * khn[:, LANES:])
        kr_ss = jnp.sum(k_r * k_r, axis=-1, keepdims=True)
        scale = math.log2(math.e) / math.sqrt(D_QK)
        for hd in range(HEADS):
            qh = st["q_all"][:, hd * D_HEAD_PAD:(hd + 1) * D_HEAD_PAD]
            q_rinv = lax.rsqrt(jnp.sum(qh * qh, axis=-1, keepdims=True) / D_QK + EPS)
            qn = qh * q_rinv * qhn
            q_out = jnp.concatenate([qn[:, :LANES], rope(qn[:, LANES:])], axis=1) * scale
            q_ref[0, hd] = q_out.astype(q_ref.dtype)
            kn = st["k_nope"][:, hd * D_NOPE:(hd + 1) * D_NOPE]
            k_rinv = lax.rsqrt((jnp.sum(kn * kn, axis=-1, keepdims=True) + kr_ss) / D_QK + EPS)
            k_out = jnp.concatenate([kn * k_rinv * khn[:, :LANES], kr_roped * k_rinv], axis=1)
            k_ref[0, hd] = k_out.astype(k_ref.dtype)

    def finish(rows, ffn):
        h = x_ref[0, rows, :] + 0.5 * ffn
        h_ref[0, rows, :] = h
        un_scr[rows, :] = _rms(h, nmix_ref[...]).astype(BF16)

    def cast_weights():
        for src, dst in zip(cast_in, cast_out):
            dst[...] = src[...].astype(dst.dtype)

    @pl.when(step < last)
    def _():
        mix_in()
        u = _rms(x_ref[0], n1_ref[...]).astype(BF16)
        _swiglu(u, wg_ref, wu_ref, wd_ref, a_scr, {3: mix_proj, 6: mix_heads, 8: cast_weights}, finish)

    @pl.when(step == last)
    def _():
        mix_in()
        mix_proj()
        mix_heads()


def _resident(shape):
    nd = len(shape)
    return pl.BlockSpec(shape, lambda *_: (0,) * nd, pipeline_mode=pl.Buffered(1))


def _cast_blocks(arr, nt):
    rows = arr.shape[0]
    steps = max(d for d in range(1, nt + 1) if rows % d == 0 and (rows // d) % BF16_SUBLANES == 0)
    return pl.BlockSpec((rows // steps, arr.shape[1]), lambda n: (jnp.minimum(n, steps - 1), 0))


def _front(x, rope_tab, w, tm, cast=()):
    bsz, s, _ = x.shape
    ns = s // tm
    nt = bsz * ns
    cast_in_specs = [_cast_blocks(a, nt) for a in cast]
    cast_out_specs = [_cast_blocks(a, nt) for a in cast]
    cur = lambda n: jnp.minimum(n, nt - 1)
    prv = lambda n: jnp.maximum(n - 1, 0)
    tok_cur = lambda n: (cur(n) // ns, cur(n) % ns, 0)
    tok_prv = lambda n: (prv(n) // ns, prv(n) % ns, 0)
    head_prv = lambda n: (prv(n) // ns, 0, prv(n) % ns, 0)
    in_specs = [
        pl.BlockSpec((1, tm, D_MODEL), tok_cur),
        pl.BlockSpec((3, tm, LANES), lambda n: (0, prv(n) % ns, 0)),
        _resident((1, D_MODEL)),
        _resident((D_MODEL, D_FF)), _resident((D_MODEL, D_FF)), _resident((D_FF, D_MODEL)),
        _resident((1, D_MODEL)), _resident((D_MODEL, Z_WIDTH)),
        _resident((1, Q_RANK)), _resident((Q_RANK, HEADS * D_HEAD_PAD)),
        _resident((1, KV_RANK)), _resident((KV_RANK, HEADS * D_NOPE)), _resident((KV_RANK, HEADS * D_V)),
        _resident((1, D_HEAD_PAD)), _resident((1, D_HEAD_PAD)),
    ] + cast_in_specs
    out_shape = [
        jax.ShapeDtypeStruct((bsz, s, D_MODEL), F32),
        jax.ShapeDtypeStruct((bsz, HEADS, s, D_HEAD_PAD), BF16),
        jax.ShapeDtypeStruct((bsz, HEADS, s, D_HEAD_PAD), BF16),
        jax.ShapeDtypeStruct((bsz, s, HEADS * D_V), BF16),
        jax.ShapeDtypeStruct((bsz, s, LRU_WIDTH), BF16),
        jax.ShapeDtypeStruct((bsz, s, LRU_WIDTH), BF16),
    ] + [jax.ShapeDtypeStruct(a.shape, BF16) for a in cast]
    out_specs = [
        pl.BlockSpec((1, tm, D_MODEL), tok_cur),
        pl.BlockSpec((1, HEADS, tm, D_HEAD_PAD), head_prv),
        pl.BlockSpec((1, HEADS, tm, D_HEAD_PAD), head_prv),
        pl.BlockSpec((1, tm, HEADS * D_V), tok_prv),
        pl.BlockSpec((1, tm, LRU_WIDTH), tok_prv),
        pl.BlockSpec((1, tm, LRU_WIDTH), tok_prv),
    ] + cast_out_specs
    return pl.pallas_call(
        functools.partial(_front_kernel, len(cast)),
        grid=(nt + 1,), in_specs=in_specs, out_specs=out_specs, out_shape=out_shape,
        scratch_shapes=[pltpu.VMEM((tm, D_FF), BF16), pltpu.VMEM((tm, D_MODEL), BF16)],
        compiler_params=pltpu.CompilerParams(
            dimension_semantics=("arbitrary",), vmem_limit_bytes=VMEM_LIMIT_BIG),
        name="front",
    )(x, rope_tab, w["n1"], w["wg1"], w["wu1"], w["wd1"], w["nmix"], w["win"],
      w["nq"], w["wuq"], w["nkv"], w["wuk"], w["wuv"], w["qhn"], w["khn"], *cast)


def _attn_body(q_ref, k_ref, v_ref, km_ref, vm_ref, o_ref, vx_scr, vmx_scr, hooks):
    hp, seq = q_ref.shape[1], q_ref.shape[2]
    for hd in range(hp):
        vcol = slice(hd * D_V, (hd + 1) * D_V)
        vx_scr[hd, :, :D_V] = v_ref[0, :, vcol]
        vx_scr[hd, :, D_V:] = jnp.ones((seq, D_V), BF16)
        vmx_scr[hd, :, :D_V] = vm_ref[0, :, vcol]
        vmx_scr[hd, :, D_V:] = jnp.ones((N_META, D_V), BF16)
    row = lax.broadcasted_iota(jnp.int32, (TQ, TQ), 0) // CHUNK
    col = lax.broadcasted_iota(jnp.int32, (TQ, TQ), 1) // CHUNK
    diag_mask = col <= row

    def scores(i, hd):
        lo, hi = i * TQ, (i + 1) * TQ
        q = q_ref[0, hd, lo:hi, :]
        s_meta = _dot_nt(q, km_ref[0, hd])
        s_diag = jnp.where(diag_mask, _dot_nt(q, k_ref[0, hd, lo:hi, :]), NEG_INF)
        s_past = _dot_nt(q, k_ref[0, hd, :lo, :]) if i > 0 else None
        return s_meta, s_diag, s_past

    def finish(i, hd, s_meta, s_diag, s_past):
        lo, hi = i * TQ, (i + 1) * TQ
        m = jnp.maximum(jnp.max(s_meta, axis=-1, keepdims=True),
                        jnp.max(s_diag, axis=-1, keepdims=True))
        if s_past is not None:
            m = jnp.maximum(m, jnp.max(s_past, axis=-1, keepdims=True))
        acc = (_dot(jnp.exp2(s_meta - m).astype(BF16), vmx_scr[hd])
               + _dot(jnp.exp2(s_diag - m).astype(BF16), vx_scr[hd, lo:hi, :]))
        if s_past is not None:
            acc = acc + _dot(jnp.exp2(s_past - m).astype(BF16), vx_scr[hd, :lo, :])
        o_ref[0, lo:hi, hd * D_V:(hd + 1) * D_V] = (acc[:, :D_V] / acc[:, D_V:]).astype(o_ref.dtype)

    items = [(i, hd) for i in range(seq // TQ) for hd in range(hp)]
    s_next = scores(*items[0])
    for n, item in enumerate(items):
        s_cur = s_next
        s_next = scores(*items[n + 1]) if n + 1 < len(items) else None
        finish(*item, *s_cur)
        for hook in hooks.get(n, ()):
            hook()


def _lru_blocks(first, step0, u_ref, g_ref, perm_ref, cw_ref, cb_ref, wa_ref, ba_ref, wx_ref, bx_ref,
                lam_ref, y_ref, ubuf, hstate):
    bsz, tt, _ = u_ref.shape
    rows = LRU_TB * bsz
    tail = (CONV_W - 1) * bsz

    def gate_tanh(xcb, w_ref, b_ref):
        parts = [_dot(xcb[:, m * GATE_GROUP:(m + 1) * GATE_GROUP], w_ref[m])
                 for m in range(LRU_WIDTH // GATE_GROUP)]
        return jnp.tanh(jnp.concatenate(parts, axis=1) + b_ref[...])

    def phases(k):
        st = {}
        t0 = k * LRU_TB

        def conv():
            perm = perm_ref[...]
            uk = jnp.concatenate([u_ref[b, t0:t0 + LRU_TB, :] for b in range(bsz)], axis=0)
            gk = jnp.concatenate([g_ref[b, t0:t0 + LRU_TB, :] for b in range(bsz)], axis=0)
            ubuf[tail:tail + rows] = _dot(perm, uk)
            st["gp"] = _dot(perm, gk)
            xc = cb_ref[...]
            for kk in range(CONV_W):
                xc = xc + cw_ref[kk:kk + 1, :] * ubuf[kk * bsz:kk * bsz + rows, :]
            ubuf[0:tail] = ubuf[rows:rows + tail]
            st["xc"] = xc
            st["xcb"] = xc.astype(BF16)

        def decay_gate():
            neg_lam = -lam_ref[...]
            softplus = jnp.maximum(neg_lam, 0.0) + jnp.log1p(jnp.exp(-jnp.abs(neg_lam)))
            half = (0.5 * C_RGLRU) * softplus
            neg_log_a = half * gate_tanh(st["xcb"], wa_ref, ba_ref) + half
            a = jnp.exp2(neg_log_a * (-math.log2(math.e)))
            m2 = jnp.tanh(neg_log_a) * (a * a + 1.0)
            mult = jnp.where(m2 > 0.0, m2 * lax.rsqrt(m2), 0.0)
            if first and k == 0:
                row = lax.broadcasted_iota(jnp.int32, (rows, LRU_WIDTH), 0)
                mult = jnp.where((row < bsz) & step0, 1.0, mult)
            st["a"], st["mult"] = a, mult

        def recurrence():
            a = st["a"]
            i_gate = 0.5 * gate_tanh(st["xcb"], wx_ref, bx_ref) + 0.5
            bt = (st["mult"] * st["xc"]) * i_gate
            h = hstate[...]
            hs = []
            for t in range(LRU_TB):
                sl = slice(t * bsz, (t + 1) * bsz)
                h = a[sl] * h + bt[sl]
                hs.append(h)
            hstate[...] = h
            st["h"] = jnp.concatenate(hs, axis=0)

        def output():
            x = st["gp"]
            c0 = math.sqrt(2.0 / math.pi)
            cdf = 0.5 * jnp.tanh(x * (c0 + (c0 * 0.044715) * (x * x))) + 0.5
            yp = (st["h"] * (x * cdf)).astype(BF16)
            y = _dot(perm_ref[...], yp).astype(y_ref.dtype)
            for b in range(bsz):
                y_ref[b, t0:t0 + LRU_TB, :] = y[b * LRU_TB:(b + 1) * LRU_TB]

        return [conv, decay_gate, recurrence, output]

    return [ph for k in range(tt // LRU_TB) for ph in phases(k)]


def _lru_kernel(first, u_ref, g_ref, hin_ref, tin_ref, perm_ref, cw_ref, cb_ref, wa_ref, ba_ref,
                wx_ref, bx_ref, lam_ref, y_ref, hout_ref, tout_ref, ubuf, hstate):
    step0 = pl.program_id(0) == 0

    @pl.when(step0)
    def _():
        ubuf[0:tin_ref.shape[0]] = tin_ref[...]
        hstate[...] = hin_ref[...]

    for blk in _lru_blocks(first, step0, u_ref, g_ref, perm_ref, cw_ref, cb_ref, wa_ref, ba_ref,
                           wx_ref, bx_ref, lam_ref, y_ref, ubuf, hstate):
        blk()
    hout_ref[...] = hstate[...]
    tout_ref[...] = ubuf[0:tout_ref.shape[0]]


def _attn_lru_kernel(q_ref, k_ref, v_ref, km_ref, vm_ref, u_ref, g_ref, hin_ref, tin_ref, perm_ref,
                     cw_ref, cb_ref, wa_ref, ba_ref, wx_ref, bx_ref, lam_ref, o_ref, y_ref,
                     vx_scr, vmx_scr, ubuf, hstate):
    step0 = (pl.program_id(0) == 0) & (pl.program_id(1) == 0)

    @pl.when(step0)
    def _():
        ubuf[0:tin_ref.shape[0]] = tin_ref[...]
        hstate[...] = hin_ref[...]

    stages = _lru_blocks(False, step0, u_ref, g_ref, perm_ref, cw_ref, cb_ref, wa_ref, ba_ref,
                         wx_ref, bx_ref, lam_ref, y_ref, ubuf, hstate)
    n_items = (q_ref.shape[2] // TQ) * q_ref.shape[1]
    assert len(stages) == n_items
    hooks = {n: [stage] for n, stage in enumerate(stages)}
    _attn_body(q_ref, k_ref, v_ref, km_ref, vm_ref, o_ref, vx_scr, vmx_scr, hooks)


def _lru_specs(bsz, tt, tok, const2, const3):
    rows = LRU_TB * bsz
    tail = (CONV_W - 1) * bsz
    ngrp = LRU_WIDTH // GATE_GROUP
    return [
        pl.BlockSpec((bsz, tt, LRU_WIDTH), tok),
        pl.BlockSpec((bsz, tt, LRU_WIDTH), tok),
        pl.BlockSpec((bsz, LRU_WIDTH), const2),
        pl.BlockSpec((tail, LRU_WIDTH), const2),
        pl.BlockSpec((rows, rows), const2),
        pl.BlockSpec((CONV_W, LRU_WIDTH), const2),
        pl.BlockSpec((1, LRU_WIDTH), const2),
        pl.BlockSpec((ngrp, GATE_GROUP, GATE_GROUP), const3),
        pl.BlockSpec((1, LRU_WIDTH), const2),
        pl.BlockSpec((ngrp, GATE_GROUP, GATE_GROUP), const3),
        pl.BlockSpec((1, LRU_WIDTH), const2),
        pl.BlockSpec((1, LRU_WIDTH), const2),
    ]


def _lru_operands(u, g, h_in, tail_in, w):
    bsz = u.shape[0]
    assert LRU_TB == bsz
    rows = LRU_TB * bsz
    eye = jnp.eye(rows, dtype=BF16).reshape(bsz, LRU_TB, rows)
    perm = eye.transpose(1, 0, 2).reshape(rows, rows)
    return (u, g, h_in, tail_in, perm, w["cw"], w["cb"], w["wa"], w["ba"], w["wx"], w["bx"], w["lam"])


def _lru_scratch(bsz):
    rows = LRU_TB * bsz
    tail = (CONV_W - 1) * bsz
    return [pltpu.VMEM((tail + rows, LRU_WIDTH), F32), pltpu.VMEM((bsz, LRU_WIDTH), F32)]


def _lru(u, g, h_in, tail_in, w, tt, first):
    bsz, s, _ = u.shape
    tail = (CONV_W - 1) * bsz
    tok = lambda i: (0, i, 0)
    const2 = lambda i: (0, 0)
    const3 = lambda i: (0, 0, 0)
    out_shape = [
        jax.ShapeDtypeStruct((bsz, s, LRU_WIDTH), BF16),
        jax.ShapeDtypeStruct((bsz, LRU_WIDTH), F32),
        jax.ShapeDtypeStruct((tail, LRU_WIDTH), F32),
    ]
    out_specs = [
        pl.BlockSpec((bsz, tt, LRU_WIDTH), tok),
        pl.BlockSpec((bsz, LRU_WIDTH), const2),
        pl.BlockSpec((tail, LRU_WIDTH), const2),
    ]
    return pl.pallas_call(
        functools.partial(_lru_kernel, first),
        grid=(s // tt,), in_specs=_lru_specs(bsz, tt, tok, const2, const3),
        out_specs=out_specs, out_shape=out_shape, scratch_shapes=_lru_scratch(bsz),
        compiler_params=pltpu.CompilerParams(
            dimension_semantics=("arbitrary",), vmem_limit_bytes=VMEM_LIMIT_SMALL),
        name="lru_meta",
    )(*_lru_operands(u, g, h_in, tail_in, w))


def _attention_lru(q, k, v, k_meta, v_meta, u, g, h_in, tail_in, w):
    bsz, _, s, _ = q.shape
    hp = ATTN_HEADS_PER_STEP
    ngroups = HEADS // hp
    tt = s // (bsz * ngroups)
    assert tt % LRU_TB == 0
    in_specs = [
        pl.BlockSpec((1, hp, s, D_HEAD_PAD), lambda b, h: (b, h, 0, 0)),
        pl.BlockSpec((1, hp, s, D_HEAD_PAD), lambda b, h: (b, h, 0, 0)),
        pl.BlockSpec((1, s, hp * D_V), lambda b, h: (b, 0, h)),
        pl.BlockSpec((1, hp, N_META, D_HEAD_PAD), lambda b, h: (0, h, 0, 0)),
        pl.BlockSpec((1, N_META, hp * D_V), lambda b, h: (0, 0, h)),
    ] + _lru_specs(bsz, tt, lambda b, h: (0, b * ngroups + h, 0), lambda b, h: (0, 0),
                   lambda b, h: (0, 0, 0))
    out_specs = [
        pl.BlockSpec((1, s, hp * D_V), lambda b, h: (b, 0, h)),
        pl.BlockSpec((bsz, tt, LRU_WIDTH), lambda b, h: (0, b * ngroups + h, 0)),
    ]
    out_shape = [
        jax.ShapeDtypeStruct((bsz, s, HEADS * D_V), BF16),
        jax.ShapeDtypeStruct((bsz, s, LRU_WIDTH), BF16),
    ]
    return pl.pallas_call(
        _attn_lru_kernel,
        grid=(bsz, ngroups), in_specs=in_specs, out_specs=out_specs, out_shape=out_shape,
        scratch_shapes=[pltpu.VMEM((hp, s, 2 * D_V), BF16), pltpu.VMEM((hp, N_META, 2 * D_V), BF16)]
        + _lru_scratch(bsz),
        compiler_params=pltpu.CompilerParams(
            dimension_semantics=("arbitrary", "arbitrary"),
            vmem_limit_bytes=VMEM_LIMIT_SMALL),
        name="attn_lru",
    )(q, k, v, k_meta, v_meta, *_lru_operands(u, g, h_in, tail_in, w))


def _back_kernel(h_ref, ym_ref, yl_ref, nao_ref, nlo_ref, wout_ref, n2_ref, wg_ref, wu_ref, wd_ref,
                 nf_ref, o_ref, a_scr, u2_scr, h2_scr):
    @pl.when(pl.program_id(0) == 0)
    def _():
        u2_scr[...] = jnp.zeros_like(u2_scr)
        h2_scr[...] = jnp.zeros_like(h2_scr)

    st = {}

    def mix_out():
        ym = _rms(ym_ref[0].astype(F32), nao_ref[...]).astype(BF16)
        yl = _rms(yl_ref[0].astype(F32), nlo_ref[...]).astype(BF16)
        half = HEADS * D_V
        h2 = h_ref[0] + _dot(ym, wout_ref[:half, :]) + _dot(yl, wout_ref[half:, :])
        st["h2"] = h2
        st["u2"] = _rms(h2, n2_ref[...]).astype(BF16)

    def finish(rows, ffn):
        o_ref[0, rows, :] = _rms(h2_scr[rows, :] + 0.5 * ffn, nf_ref[...])

    _swiglu(u2_scr[...], wg_ref, wu_ref, wd_ref, a_scr, {4: mix_out}, finish)
    u2_scr[...] = st["u2"]
    h2_scr[...] = st["h2"]


def _back(h1, y_mla, y_lru, w, tm):
    bsz, s, _ = h1.shape
    ns = s // tm
    nt = bsz * ns
    cur = lambda n: jnp.minimum(n, nt - 1)
    prv = lambda n: jnp.maximum(n - 1, 0)
    tok_cur = lambda n: (cur(n) // ns, cur(n) % ns, 0)
    tok_prv = lambda n: (prv(n) // ns, prv(n) % ns, 0)
    in_specs = [
        pl.BlockSpec((1, tm, D_MODEL), tok_cur),
        pl.BlockSpec((1, tm, HEADS * D_V), tok_cur),
        pl.BlockSpec((1, tm, LRU_WIDTH), tok_cur),
        _resident((1, HEADS * D_V)), _resident((1, LRU_WIDTH)),
        _resident((HEADS * D_V + LRU_WIDTH, D_MODEL)),
        _resident((1, D_MODEL)),
        _resident((D_MODEL, D_FF)), _resident((D_MODEL, D_FF)), _resident((D_FF, D_MODEL)),
        _resident((1, D_MODEL)),
    ]
    return pl.pallas_call(
        _back_kernel,
        grid=(nt + 1,), in_specs=in_specs,
        out_specs=pl.BlockSpec((1, tm, D_MODEL), tok_prv),
        out_shape=jax.ShapeDtypeStruct((bsz, s, D_MODEL), F32),
        scratch_shapes=[pltpu.VMEM((tm, D_FF), BF16), pltpu.VMEM((tm, D_MODEL), BF16),
                        pltpu.VMEM((tm, D_MODEL), F32)],
        compiler_params=pltpu.CompilerParams(
            dimension_semantics=("arbitrary",), vmem_limit_bytes=VMEM_LIMIT_BIG),
        name="back",
    )(h1, y_mla, y_lru, w["nao"], w["nlo"], w["wout"], w["n2"], w["wg2"], w["wu2"], w["wd2"], w["nf"])


def _rope_table(pos):
    half = D_ROPE // 2
    inv_freq = ROPE_THETA ** (-jnp.arange(0, half, dtype=F32) / half)
    ang = pos.astype(F32)[:, None] * inv_freq[None, :]
    cos, sin = jnp.cos(ang), jnp.sin(ang)
    zeros = jnp.zeros_like(cos)
    pad = jnp.zeros((pos.shape[0], LANES - D_ROPE), F32)
    return jnp.stack([
        jnp.concatenate([cos, cos, pad], axis=1),
        jnp.concatenate([-sin, zeros, pad], axis=1),
        jnp.concatenate([zeros, sin, pad], axis=1),
    ])


def _block_diag_groups(wgt):
    per = GATE_GROUP // LRU_BLOCK
    w4 = wgt.reshape(LRU_BLOCKS // per, per, LRU_BLOCK, LRU_BLOCK)
    eye = jnp.eye(per, dtype=wgt.dtype)
    return jnp.einsum('mnij,nk->mnikj', w4, eye).reshape(LRU_BLOCKS // per, GATE_GROUP, GATE_GROUP)


def kernel(x, meta_tokens, ffn1_norm, ffn1_w_gate, ffn1_w_up, ffn1_w_down, mix_norm, w_in, q_latent_norm, w_uq, kv_latent_norm, w_uk, w_uv, q_head_norm, k_head_norm, conv_w, conv_b, gate_a_w, gate_a_b, gate_x_w, gate_x_b, lru_lambda, attn_out_norm, lru_out_norm, w_out, ffn2_norm, ffn2_w_gate, ffn2_w_up, ffn2_w_down, final_norm):
    l = 0
    row = lambda a: a[l].reshape(1, -1).astype(F32)
    bf = lambda a: a[l].astype(BF16)

    o2 = Q_RANK + KV_RANK
    o3 = o2 + D_ROPE
    win = w_in[l]
    win_p = jnp.concatenate(
        [win[:, :o3], jnp.zeros((D_MODEL, LANES - D_ROPE), win.dtype), win[:, o3:]], axis=1)
    wuq_p = jnp.pad(w_uq[l].reshape(Q_RANK, HEADS, D_QK),
                    ((0, 0), (0, 0), (0, D_HEAD_PAD - D_QK))).reshape(Q_RANK, HEADS * D_HEAD_PAD)
    pad_head = lambda a: jnp.pad(a[l].astype(F32), (0, D_HEAD_PAD - D_QK)).reshape(1, D_HEAD_PAD)
    w = {
        "n1": row(ffn1_norm), "wg1": bf(ffn1_w_gate), "wu1": bf(ffn1_w_up), "wd1": bf(ffn1_w_down),
        "nmix": row(mix_norm), "win": win_p.astype(BF16),
        "nq": row(q_latent_norm), "wuq": wuq_p.astype(BF16),
        "nkv": row(kv_latent_norm), "wuk": bf(w_uk), "wuv": bf(w_uv),
        "qhn": pad_head(q_head_norm), "khn": pad_head(k_head_norm),
        "cw": conv_w[l].astype(F32), "cb": row(conv_b),
        "wa": (0.5 * _block_diag_groups(gate_a_w[l])).astype(BF16), "ba": 0.5 * row(gate_a_b),
        "wx": (0.5 * _block_diag_groups(gate_x_w[l])).astype(BF16), "bx": 0.5 * row(gate_x_b),
        "lam": row(lru_lambda),
        "nao": row(attn_out_norm), "nlo": row(lru_out_norm), "wout": bf(w_out),
        "n2": row(ffn2_norm),
        "nf": row(final_norm),
    }

    meta = meta_tokens.astype(x.dtype)[None]
    rope_meta = _rope_table(jnp.arange(N_META))
    _, _, k_meta, v_meta, u_meta, g_meta = _front(meta, rope_meta, w, N_META)[:6]
    bsz = x.shape[0]
    rep = lambda a: jnp.broadcast_to(a, (bsz,) + a.shape[1:])
    zero_h = jnp.zeros((bsz, LRU_WIDTH), F32)
    zero_tail = jnp.zeros(((CONV_W - 1) * bsz, LRU_WIDTH), F32)
    _, h_meta, tail_meta = _lru(rep(u_meta), rep(g_meta), zero_h, zero_tail, w, N_META, True)

    rope_main = _rope_table(N_META + jnp.arange(SEQ))
    ffn2_f32 = (ffn2_w_gate[l], ffn2_w_up[l], ffn2_w_down[l])
    h1, q, k, v, u, g, w["wg2"], w["wu2"], w["wd2"] = _front(x, rope_main, w, TM, ffn2_f32)
    y_mla, y_lru = _attention_lru(q, k, v, k_meta, v_meta, u, g, h_meta, tail_meta, w)
    return _back(h1, y_mla, y_lru, w, TM)
```

```python
import functools
import math

import jax
import jax.numpy as jnp
import numpy as np
from jax import lax
from jax.experimental import pallas as pl
from jax.experimental.pallas import tpu as pltpu

D_MODEL = 1024
SEQ = 2048
N_META = 16
CHUNK = 64
HEADS = 4
D_NOPE = 128
D_ROPE = 64
D_QK = D_NOPE + D_ROPE
D_V = 128
KV_RANK = 256
Q_RANK = 384
ROPE_THETA = 10000.0
LRU_WIDTH = 512
LRU_BLOCKS = 8
LRU_BLOCK = 64
CONV_W = 4
C_RGLRU = 8.0
D_FF = 2816
EPS = 1e-6
NEG_INF = -1e30

LANES = 128
BF16_SUBLANES = 16
D_HEAD_PAD = 2 * LANES
Z_WIDTH = Q_RANK + KV_RANK + LANES + 2 * LRU_WIDTH
GATE_GROUP = 256

TM = 512
FF_CHUNK = 256
DOWN_ROWS = 256
TQ = 256
ATTN_HEADS_PER_STEP = 2
LRU_TB = 16
VMEM_LIMIT_BIG = 56 * 1024 * 1024
VMEM_LIMIT_SMALL = 40 * 1024 * 1024

F32 = jnp.float32
BF16 = jnp.bfloat16


def _rms(x, g):
    ms = jnp.mean(x * x, axis=-1, keepdims=True)
    return x * lax.rsqrt(ms + EPS) * g


def _dot(a, b):
    return jnp.dot(a, b, preferred_element_type=F32)


def _dot_nt(a, b):
    return lax.dot_general(a, b, (((1,), (1,)), ((), ())), preferred_element_type=F32)


def _swiglu(u, wg_ref, wu_ref, wd_ref, a_ref, hooks, finish):
    for c in range(D_FF // FF_CHUNK):
        sl = slice(c * FF_CHUNK, (c + 1) * FF_CHUNK)
        gate = _dot(u, wg_ref[:, sl])
        up = _dot(u, wu_ref[:, sl])
        a_ref[:, sl] = (jax.nn.silu(gate) * up).astype(BF16)
        if c in hooks:
            hooks[c]()
    tm = a_ref.shape[0]
    blk = min(tm, DOWN_ROWS)
    for r in range(tm // blk):
        rows = slice(r * blk, (r + 1) * blk)
        finish(rows, _dot(a_ref[rows, :], wd_ref[...]))


def _front_kernel(n_cast, x_ref, rope_ref, n1_ref, wg_ref, wu_ref, wd_ref, nmix_ref, win_ref,
                  nq_ref, wuq_ref, nkv_ref, wuk_ref, wuv_ref, qhn_ref, khn_ref, *rest):
    cast_in, rest = rest[:n_cast], rest[n_cast:]
    h_ref, q_ref, k_ref, v_ref, u_ref, g_ref = rest[:6]
    cast_out, (a_scr, un_scr) = rest[6:6 + n_cast], rest[6 + n_cast:]
    step = pl.program_id(0)
    last = pl.num_programs(0) - 1

    @pl.when(step == 0)
    def _():
        un_scr[...] = jnp.zeros_like(un_scr)

    st = {}

    def mix_in():
        z = _dot(un_scr[...], win_ref[...])
        o1 = Q_RANK
        o2 = o1 + KV_RANK
        o3 = o2 + LANES
        o4 = o3 + LRU_WIDTH
        st["c_q"], st["c_kv"], st["k_r"] = z[:, :o1], z[:, o1:o2], z[:, o2:o3]
        u_ref[0] = z[:, o3:o4].astype(u_ref.dtype)
        g_ref[0] = z[:, o4:].astype(g_ref.dtype)

    def mix_proj():
        st["q_all"] = _dot(_rms(st["c_q"], nq_ref[...]).astype(BF16), wuq_ref[...])
        ckv = _rms(st["c_kv"], nkv_ref[...]).astype(BF16)
        st["k_nope"] = _dot(ckv, wuk_ref[...])
        v_ref[0] = _dot(ckv, wuv_ref[...]).astype(v_ref.dtype)

    def mix_heads():
        cos, sin_lo, sin_hi = rope_ref[0], rope_ref[1], rope_ref[2]

        def rope(x):
            return (x * cos + pltpu.roll(x, LANES - D_ROPE // 2, 1) * sin_lo
                    + pltpu.roll(x, D_ROPE // 2, 1) * sin_hi)

        qhn = qhn_ref[...]
        khn = khn_ref[...]
        k_r = st["k_r"]
        kr_roped = rope(k_r * khn[:, LANES:])
        kr_ss = jnp.sum(k_r * k_r, axis=-1, keepdims=True)
        scale = math.log2(math.e) / math.sqrt(D_QK)
        for hd in range(HEADS):
            qh = st["q_all"][:, hd * D_HEAD_PAD:(hd + 1) * D_HEAD_PAD]
            q_rinv = lax.rsqrt(jnp.sum(qh * qh, axis=-1, keepdims=True) / D_QK + EPS)
            qn = qh * q_rinv * qhn
            q_out = jnp.concatenate([qn[:, :LANES], rope(qn[:, LANES:])], axis=1) * scale
            q_ref[0, hd] = q_out.astype(q_ref.dtype)
            kn = st["k_nope"][:, hd * D_NOPE:(hd + 1) * D_NOPE]
            k_rinv = lax.rsqrt((jnp.sum(kn * kn, axis=-1, keepdims=True) + kr_ss) / D_QK + EPS)
            k_out = jnp.concatenate([kn * k_rinv * khn[:, :LANES], kr_roped * k_rinv], axis=1)
            k_ref[0, hd] = k_out.astype(k_ref.dtype)

    def finish(rows, ffn):
        h = x_ref[0, rows, :] + 0.5 * ffn
        h_ref[0, rows, :] = h
        un_scr[rows, :] = _rms(h, nmix_ref[...]).astype(BF16)

    def cast_weights():
        for src, dst in zip(cast_in, cast_out):
            dst[...] = src[...].astype(dst.dtype)

    @pl.when(step < last)
    def _():
        mix_in()
        u = _rms(x_ref[0], n1_ref[...]).astype(BF16)
        _swiglu(u, wg_ref, wu_ref, wd_ref, a_scr, {3: mix_proj, 6: mix_heads, 8: cast_weights}, finish)

    @pl.when(step == last)
    def _():
        mix_in()
        mix_proj()
        mix_heads()


def _resident(shape):
    nd = len(shape)
    return pl.BlockSpec(shape, lambda *_: (0,) * nd, pipeline_mode=pl.Buffered(1))


def _cast_blocks(arr, nt):
    rows = arr.shape[0]
    steps = max(d for d in range(1, nt + 1) if rows % d == 0 and (rows // d) % BF16_SUBLANES == 0)
    return pl.BlockSpec((rows // steps, arr.shape[1]), lambda n: (jnp.minimum(n, steps - 1), 0))


def _front(x, rope_tab, w, tm, cast=()):
    bsz, s, _ = x.shape
    ns = s // tm
    nt = bsz * ns
    cast_in_specs = [_cast_blocks(a, nt) for a in cast]
    cast_out_specs = [_cast_blocks(a, nt) for a in cast]
    cur = lambda n: jnp.minimum(n, nt - 1)
    prv = lambda n: jnp.maximum(n - 1, 0)
    tok_cur = lambda n: (cur(n) // ns, cur(n) % ns, 0)
    tok_prv = lambda n: (prv(n) // ns, prv(n) % ns, 0)
    head_prv = lambda n: (prv(n) // ns, 0, prv(n) % ns, 0)
    in_specs = [
        pl.BlockSpec((1, tm, D_MODEL), tok_cur),
        pl.BlockSpec((3, tm, LANES), lambda n: (0, prv(n) % ns, 0)),
        _resident((1, D_MODEL)),
        _resident((D_MODEL, D_FF)), _resident((D_MODEL, D_FF)), _resident((D_FF, D_MODEL)),
        _resident((1, D_MODEL)), _resident((D_MODEL, Z_WIDTH)),
        _resident((1, Q_RANK)), _resident((Q_RANK, HEADS * D_HEAD_PAD)),
        _resident((1, KV_RANK)), _resident((KV_RANK, HEADS * D_NOPE)), _resident((KV_RANK, HEADS * D_V)),
        _resident((1, D_HEAD_PAD)), _resident((1, D_HEAD_PAD)),
    ] + cast_in_specs
    out_shape = [
        jax.ShapeDtypeStruct((bsz, s, D_MODEL), F32),
        jax.ShapeDtypeStruct((bsz, HEADS, s, D_HEAD_PAD), BF16),
        jax.ShapeDtypeStruct((bsz, HEADS, s, D_HEAD_PAD), BF16),
        jax.ShapeDtypeStruct((bsz, s, HEADS * D_V), BF16),
        jax.ShapeDtypeStruct((bsz, s, LRU_WIDTH), BF16),
        jax.ShapeDtypeStruct((bsz, s, LRU_WIDTH), BF16),
    ] + [jax.ShapeDtypeStruct(a.shape, BF16) for a in cast]
    out_specs = [
        pl.BlockSpec((1, tm, D_MODEL), tok_cur),
        pl.BlockSpec((1, HEADS, tm, D_HEAD_PAD), head_prv),
        pl.BlockSpec((1, HEADS, tm, D_HEAD_PAD), head_prv),
        pl.BlockSpec((1, tm, HEADS * D_V), tok_prv),
        pl.BlockSpec((1, tm, LRU_WIDTH), tok_prv),
        pl.BlockSpec((1, tm, LRU_WIDTH), tok_prv),
    ] + cast_out_specs
    return pl.pallas_call(
        functools.partial(_front_kernel, len(cast)),
        grid=(nt + 1,), in_specs=in_specs, out_specs=out_specs, out_shape=out_shape,
        scratch_shapes=[pltpu.VMEM((tm, D_FF), BF16), pltpu.VMEM((tm, D_MODEL), BF16)],
        compiler_params=pltpu.CompilerParams(
            dimension_semantics=("arbitrary",), vmem_limit_bytes=VMEM_LIMIT_BIG),
        name="front",
    )(x, rope_tab, w["n1"], w["wg1"], w["wu1"], w["wd1"], w["nmix"], w["win"],
      w["nq"], w["wuq"], w["nkv"], w["wuk"], w["wuv"], w["qhn"], w["khn"], *cast)


def _attn_body(q_ref, k_ref, v_ref, km_ref, vm_ref, o_ref, kx_scr, vx_scr, hooks):
    hp, seq = q_ref.shape[1], q_ref.shape[2]
    n_tiles = seq // TQ + 1
    end = N_META + seq
    for hd in range(hp):
        vcol = slice(hd * D_V, (hd + 1) * D_V)
        kx_scr[hd, :N_META, :] = km_ref[0, hd]
        kx_scr[hd, N_META:end, :] = k_ref[0, hd]
        vx_scr[hd, :N_META, :D_V] = vm_ref[0, :, vcol]
        vx_scr[hd, N_META:end, :D_V] = v_ref[0, :, vcol]
        vx_scr[hd, end:, :D_V] = jnp.zeros((n_tiles * TQ - end, D_V), BF16)
        kx_scr[hd, end:, :] = jnp.zeros((n_tiles * TQ - end, D_HEAD_PAD), BF16)
        vx_scr[hd, :, D_V:] = jnp.ones((n_tiles * TQ, D_V), BF16)

    def rows_of(t):
        return max(0, TQ * t - CHUNK), min(seq, TQ * (t + 1) - CHUNK)

    def staircase(t):
        lo, hi = rows_of(t)
        qchunk = (lo + lax.broadcasted_iota(jnp.int32, (hi - lo, TQ), 0)) // CHUNK
        kpos = TQ * t + lax.broadcasted_iota(jnp.int32, (hi - lo, TQ), 1)
        return kpos < N_META + CHUNK * (qchunk + 1)

    def scores(t, hd):
        lo, hi = rows_of(t)
        q = q_ref[0, hd, lo:hi, :]
        s_last = jnp.where(staircase(t), _dot_nt(q, kx_scr[hd, TQ * t:TQ * (t + 1), :]), NEG_INF)
        s_past = _dot_nt(q, kx_scr[hd, :TQ * t, :]) if t > 0 else None
        return s_last, s_past

    def finish(t, hd, s_last, s_past):
        lo, hi = rows_of(t)
        m = jnp.max(s_last, axis=-1, keepdims=True)
        if s_past is not None:
            m = jnp.maximum(m, jnp.max(s_past, axis=-1, keepdims=True))
        acc = _dot(jnp.exp2(s_last - m).astype(BF16), vx_scr[hd, TQ * t:TQ * (t + 1), :])
        if s_past is not None:
            acc = acc + _dot(jnp.exp2(s_past - m).astype(BF16), vx_scr[hd, :TQ * t, :])
        o_ref[0, lo:hi, hd * D_V:(hd + 1) * D_V] = (acc[:, :D_V] / acc[:, D_V:]).astype(o_ref.dtype)

    order = [t for pair in zip(range(n_tiles - 1, -1, -1), range(n_tiles)) for t in pair][:n_tiles]
    items = [(t, hd) for t in order for hd in range(hp)]
    s_next = scores(*items[0])
    for n, item in enumerate(items):
        s_cur = s_next
        s_next = scores(*items[n + 1]) if n + 1 < len(items) else None
        finish(*item, *s_cur)
        for hook in hooks.get(n, ()):
            hook()


def _lru_blocks(first, step0, u_ref, g_ref, perm_ref, cw_ref, cb_ref, wa_ref, ba_ref, wx_ref, bx_ref,
                lam_ref, y_ref, ubuf, hstate):
    bsz, tt, _ = u_ref.shape
    rows = LRU_TB * bsz
    tail = (CONV_W - 1) * bsz

    def gate_tanh(xcb, w_ref, b_ref):
        parts = [_dot(xcb[:, m * GATE_GROUP:(m + 1) * GATE_GROUP], w_ref[m])
                 for m in range(LRU_WIDTH // GATE_GROUP)]
        return jnp.tanh(jnp.concatenate(parts, axis=1) + b_ref[...])

    def phases(k):
        st = {}
        t0 = k * LRU_TB

        def conv():
            perm = perm_ref[...]
            uk = jnp.concatenate([u_ref[b, t0:t0 + LRU_TB, :] for b in range(bsz)], axis=0)
            gk = jnp.concatenate([g_ref[b, t0:t0 + LRU_TB, :] for b in range(bsz)], axis=0)
            ubuf[tail:tail + rows] = _dot(perm, uk)
            st["gp"] = _dot(perm, gk)
            xc = cb_ref[...]
            for kk in range(CONV_W):
                xc = xc + cw_ref[kk:kk + 1, :] * ubuf[kk * bsz:kk * bsz + rows, :]
            ubuf[0:tail] = ubuf[rows:rows + tail]
            st["xc"] = xc
            st["xcb"] = xc.astype(BF16)

        def decay_gate():
            neg_lam = -lam_ref[...]
            softplus = jnp.maximum(neg_lam, 0.0) + jnp.log1p(jnp.exp(-jnp.abs(neg_lam)))
            half = (0.5 * C_RGLRU) * softplus
            neg_log_a = half * gate_tanh(st["xcb"], wa_ref, ba_ref) + half
            a = jnp.exp2(neg_log_a * (-math.log2(math.e)))
            m2 = jnp.tanh(neg_log_a) * (a * a + 1.0)
            mult = jnp.where(m2 > 0.0, m2 * lax.rsqrt(m2), 0.0)
            if first and k == 0:
                row = lax.broadcasted_iota(jnp.int32, (rows, LRU_WIDTH), 0)
                mult = jnp.where((row < bsz) & step0, 1.0, mult)
            st["a"], st["mult"] = a, mult

        def recurrence():
            a = st["a"]
            i_gate = 0.5 * gate_tanh(st["xcb"], wx_ref, bx_ref) + 0.5
            bt = (st["mult"] * st["xc"]) * i_gate
            h = hstate[...]
            hs = []
            for t in range(LRU_TB):
                sl = slice(t * bsz, (t + 1) * bsz)
                h = a[sl] * h + bt[sl]
                hs.append(h)
            hstate[...] = h
            st["h"] = jnp.concatenate(hs, axis=0)

        def output():
            x = st["gp"]
            c0 = math.sqrt(2.0 / math.pi)
            cdf = 0.5 * jnp.tanh(x * (c0 + (c0 * 0.044715) * (x * x))) + 0.5
            yp = (st["h"] * (x * cdf)).astype(BF16)
            y = _dot(perm_ref[...], yp).astype(y_ref.dtype)
            for b in range(bsz):
                y_ref[b, t0:t0 + LRU_TB, :] = y[b * LRU_TB:(b + 1) * LRU_TB]

        return [conv, decay_gate, recurrence, output]

    return [ph for k in range(tt // LRU_TB) for ph in phases(k)]


def _lru_kernel(first, u_ref, g_ref, hin_ref, tin_ref, perm_ref, cw_ref, cb_ref, wa_ref, ba_ref,
                wx_ref, bx_ref, lam_ref, y_ref, hout_ref, tout_ref, ubuf, hstate):
    step0 = pl.program_id(0) == 0

    @pl.when(step0)
    def _():
        ubuf[0:tin_ref.shape[0]] = tin_ref[...]
        hstate[...] = hin_ref[...]

    for blk in _lru_blocks(first, step0, u_ref, g_ref, perm_ref, cw_ref, cb_ref, wa_ref, ba_ref,
                           wx_ref, bx_ref, lam_ref, y_ref, ubuf, hstate):
        blk()
    hout_ref[...] = hstate[...]
    tout_ref[...] = ubuf[0:tout_ref.shape[0]]


def _attn_lru_kernel(q_ref, k_ref, v_ref, km_ref, vm_ref, u_ref, g_ref, hin_ref, tin_ref, perm_ref,
                     cw_ref, cb_ref, wa_ref, ba_ref, wx_ref, bx_ref, lam_ref, o_ref, y_ref,
                     kx_scr, vx_scr, ubuf, hstate):
    step0 = (pl.program_id(0) == 0) & (pl.program_id(1) == 0)

    @pl.when(step0)
    def _():
        ubuf[0:tin_ref.shape[0]] = tin_ref[...]
        hstate[...] = hin_ref[...]

    stages = _lru_blocks(False, step0, u_ref, g_ref, perm_ref, cw_ref, cb_ref, wa_ref, ba_ref,
                         wx_ref, bx_ref, lam_ref, y_ref, ubuf, hstate)
    n_items = (q_ref.shape[2] // TQ + 1) * q_ref.shape[1]
    assert len(stages) <= n_items
    hooks = {n: [stage] for n, stage in enumerate(stages)}
    _attn_body(q_ref, k_ref, v_ref, km_ref, vm_ref, o_ref, kx_scr, vx_scr, hooks)


def _lru_specs(bsz, tt, tok, const2, const3):
    rows = LRU_TB * bsz
    tail = (CONV_W - 1) * bsz
    ngrp = LRU_WIDTH // GATE_GROUP
    return [
        pl.BlockSpec((bsz, tt, LRU_WIDTH), tok),
        pl.BlockSpec((bsz, tt, LRU_WIDTH), tok),
        pl.BlockSpec((bsz, LRU_WIDTH), const2),
        pl.BlockSpec((tail, LRU_WIDTH), const2),
        pl.BlockSpec((rows, rows), const2),
        pl.BlockSpec((CONV_W, LRU_WIDTH), const2),
        pl.BlockSpec((1, LRU_WIDTH), const2),
        pl.BlockSpec((ngrp, GATE_GROUP, GATE_GROUP), const3),
        pl.BlockSpec((1, LRU_WIDTH), const2),
        pl.BlockSpec((ngrp, GATE_GROUP, GATE_GROUP), const3),
        pl.BlockSpec((1, LRU_WIDTH), const2),
        pl.BlockSpec((1, LRU_WIDTH), const2),
    ]


def _lru_operands(u, g, h_in, tail_in, w):
    bsz = u.shape[0]
    assert LRU_TB == bsz
    rows = LRU_TB * bsz
    eye = jnp.eye(rows, dtype=BF16).reshape(bsz, LRU_TB, rows)
    perm = eye.transpose(1, 0, 2).reshape(rows, rows)
    return (u, g, h_in, tail_in, perm, w["cw"], w["cb"], w["wa"], w["ba"], w["wx"], w["bx"], w["lam"])


def _lru_scratch(bsz):
    rows = LRU_TB * bsz
    tail = (CONV_W - 1) * bsz
    return [pltpu.VMEM((tail + rows, LRU_WIDTH), F32), pltpu.VMEM((bsz, LRU_WIDTH), F32)]


def _lru(u, g, h_in, tail_in, w, tt, first):
    bsz, s, _ = u.shape
    tail = (CONV_W - 1) * bsz
    tok = lambda i: (0, i, 0)
    const2 = lambda i: (0, 0)
    const3 = lambda i: (0, 0, 0)
    out_shape = [
        jax.ShapeDtypeStruct((bsz, s, LRU_WIDTH), BF16),
        jax.ShapeDtypeStruct((bsz, LRU_WIDTH), F32),
        jax.ShapeDtypeStruct((tail, LRU_WIDTH), F32),
    ]
    out_specs = [
        pl.BlockSpec((bsz, tt, LRU_WIDTH), tok),
        pl.BlockSpec((bsz, LRU_WIDTH), const2),
        pl.BlockSpec((tail, LRU_WIDTH), const2),
    ]
    return pl.pallas_call(
        functools.partial(_lru_kernel, first),
        grid=(s // tt,), in_specs=_lru_specs(bsz, tt, tok, const2, const3),
        out_specs=out_specs, out_shape=out_shape, scratch_shapes=_lru_scratch(bsz),
        compiler_params=pltpu.CompilerParams(
            dimension_semantics=("arbitrary",), vmem_limit_bytes=VMEM_LIMIT_SMALL),
        name="lru_meta",
    )(*_lru_operands(u, g, h_in, tail_in, w))


def _attention_lru(q, k, v, k_meta, v_meta, u, g, h_in, tail_in, w):
    bsz, _, s, _ = q.shape
    hp = ATTN_HEADS_PER_STEP
    ngroups = HEADS // hp
    tt = s // (bsz * ngroups)
    assert tt % LRU_TB == 0
    in_specs = [
        pl.BlockSpec((1, hp, s, D_HEAD_PAD), lambda b, h: (b, h, 0, 0)),
        pl.BlockSpec((1, hp, s, D_HEAD_PAD), lambda b, h: (b, h, 0, 0)),
        pl.BlockSpec((1, s, hp * D_V), lambda b, h: (b, 0, h)),
        pl.BlockSpec((1, hp, N_META, D_HEAD_PAD), lambda b, h: (0, h, 0, 0)),
        pl.BlockSpec((1, N_META, hp * D_V), lambda b, h: (0, 0, h)),
    ] + _lru_specs(bsz, tt, lambda b, h: (0, b * ngroups + h, 0), lambda b, h: (0, 0),
                   lambda b, h: (0, 0, 0))
    out_specs = [
        pl.BlockSpec((1, s, hp * D_V), lambda b, h: (b, 0, h)),
        pl.BlockSpec((bsz, tt, LRU_WIDTH), lambda b, h: (0, b * ngroups + h, 0)),
    ]
    out_shape = [
        jax.ShapeDtypeStruct((bsz, s, HEADS * D_V), BF16),
        jax.ShapeDtypeStruct((bsz, s, LRU_WIDTH), BF16),
    ]
    return pl.pallas_call(
        _attn_lru_kernel,
        grid=(bsz, ngroups), in_specs=in_specs, out_specs=out_specs, out_shape=out_shape,
        scratch_shapes=[pltpu.VMEM((hp, s + TQ, D_HEAD_PAD), BF16), pltpu.VMEM((hp, s + TQ, 2 * D_V), BF16)]
        + _lru_scratch(bsz),
        compiler_params=pltpu.CompilerParams(
            dimension_semantics=("arbitrary", "arbitrary"),
            vmem_limit_bytes=VMEM_LIMIT_SMALL),
        name="attn_lru",
    )(q, k, v, k_meta, v_meta, *_lru_operands(u, g, h_in, tail_in, w))


def _back_kernel(h_ref, ym_ref, yl_ref, nao_ref, nlo_ref, wout_ref, n2_ref, wg_ref, wu_ref, wd_ref,
                 nf_ref, o_ref, a_scr, u2_scr, h2_scr):
    @pl.when(pl.program_id(0) == 0)
    def _():
        u2_scr[...] = jnp.zeros_like(u2_scr)
        h2_scr[...] = jnp.zeros_like(h2_scr)

    st = {}

    def mix_out():
        ym = _rms(ym_ref[0].astype(F32), nao_ref[...]).astype(BF16)
        yl = _rms(yl_ref[0].astype(F32), nlo_ref[...]).astype(BF16)
        half = HEADS * D_V
        h2 = h_ref[0] + _dot(ym, wout_ref[:half, :]) + _dot(yl, wout_ref[half:, :])
        st["h2"] = h2
        st["u2"] = _rms(h2, n2_ref[...]).astype(BF16)

    def finish(rows, ffn):
        o_ref[0, rows, :] = _rms(h2_scr[rows, :] + 0.5 * ffn, nf_ref[...])

    _swiglu(u2_scr[...], wg_ref, wu_ref, wd_ref, a_scr, {4: mix_out}, finish)
    u2_scr[...] = st["u2"]
    h2_scr[...] = st["h2"]


def _back(h1, y_mla, y_lru, w, tm):
    bsz, s, _ = h1.shape
    ns = s // tm
    nt = bsz * ns
    cur = lambda n: jnp.minimum(n, nt - 1)
    prv = lambda n: jnp.maximum(n - 1, 0)
    tok_cur = lambda n: (cur(n) // ns, cur(n) % ns, 0)
    tok_prv = lambda n: (prv(n) // ns, prv(n) % ns, 0)
    in_specs = [
        pl.BlockSpec((1, tm, D_MODEL), tok_cur),
        pl.BlockSpec((1, tm, HEADS * D_V), tok_cur),
        pl.BlockSpec((1, tm, LRU_WIDTH), tok_cur),
        _resident((1, HEADS * D_V)), _resident((1, LRU_WIDTH)),
        _resident((HEADS * D_V + LRU_WIDTH, D_MODEL)),
        _resident((1, D_MODEL)),
        _resident((D_MODEL, D_FF)), _resident((D_MODEL, D_FF)), _resident((D_FF, D_MODEL)),
        _resident((1, D_MODEL)),
    ]
    return pl.pallas_call(
        _back_kernel,
        grid=(nt + 1,), in_specs=in_specs,
        out_specs=pl.BlockSpec((1, tm, D_MODEL), tok_prv),
        out_shape=jax.ShapeDtypeStruct((bsz, s, D_MODEL), F32),
        scratch_shapes=[pltpu.VMEM((tm, D_FF), BF16), pltpu.VMEM((tm, D_MODEL), BF16),
                        pltpu.VMEM((tm, D_MODEL), F32)],
        compiler_params=pltpu.CompilerParams(
            dimension_semantics=("arbitrary",), vmem_limit_bytes=VMEM_LIMIT_BIG),
        name="back",
    )(h1, y_mla, y_lru, w["nao"], w["nlo"], w["wout"], w["n2"], w["wg2"], w["wu2"], w["wd2"], w["nf"])


def _rope_table(first_pos, n):
    half = D_ROPE // 2
    inv_freq = np.float32(ROPE_THETA) ** (-np.arange(0, half, dtype=np.float32) / np.float32(half))
    ang = np.arange(first_pos, first_pos + n, dtype=np.float32)[:, None] * inv_freq[None, :]
    cos, sin = np.cos(ang), np.sin(ang)
    zeros = np.zeros_like(cos)
    pad = np.zeros((n, LANES - D_ROPE), np.float32)
    return jnp.asarray(np.stack([
        np.concatenate([cos, cos, pad], axis=1),
        np.concatenate([-sin, zeros, pad], axis=1),
        np.concatenate([zeros, sin, pad], axis=1),
    ]).astype(np.float32))


def _block_diag_groups(wgt):
    per = GATE_GROUP // LRU_BLOCK
    w4 = wgt.reshape(LRU_BLOCKS // per, per, LRU_BLOCK, LRU_BLOCK)
    eye = jnp.eye(per, dtype=wgt.dtype)
    return jnp.einsum('mnij,nk->mnikj', w4, eye).reshape(LRU_BLOCKS // per, GATE_GROUP, GATE_GROUP)


def kernel(x, meta_tokens, ffn1_norm, ffn1_w_gate, ffn1_w_up, ffn1_w_down, mix_norm, w_in, q_latent_norm, w_uq, kv_latent_norm, w_uk, w_uv, q_head_norm, k_head_norm, conv_w, conv_b, gate_a_w, gate_a_b, gate_x_w, gate_x_b, lru_lambda, attn_out_norm, lru_out_norm, w_out, ffn2_norm, ffn2_w_gate, ffn2_w_up, ffn2_w_down, final_norm):
    l = 0
    row = lambda a: a[l].reshape(1, -1).astype(F32)
    bf = lambda a: a[l].astype(BF16)

    o2 = Q_RANK + KV_RANK
    o3 = o2 + D_ROPE
    win = w_in[l].astype(BF16)
    win_p = jnp.concatenate(
        [win[:, :o3], jnp.zeros((D_MODEL, LANES - D_ROPE), win.dtype), win[:, o3:]], axis=1)
    wuq_p = jnp.pad(w_uq[l].reshape(Q_RANK, HEADS, D_QK),
                    ((0, 0), (0, 0), (0, D_HEAD_PAD - D_QK))).reshape(Q_RANK, HEADS * D_HEAD_PAD)
    pad_head = lambda a: jnp.pad(a[l].astype(F32), (0, D_HEAD_PAD - D_QK)).reshape(1, D_HEAD_PAD)
    w = {
        "n1": row(ffn1_norm), "wg1": bf(ffn1_w_gate), "wu1": bf(ffn1_w_up), "wd1": bf(ffn1_w_down),
        "nmix": row(mix_norm), "win": win_p.astype(BF16),
        "nq": row(q_latent_norm), "wuq": wuq_p.astype(BF16),
        "nkv": row(kv_latent_norm), "wuk": bf(w_uk), "wuv": bf(w_uv),
        "qhn": pad_head(q_head_norm), "khn": pad_head(k_head_norm),
        "cw": conv_w[l].astype(F32), "cb": row(conv_b),
        "wa": (0.5 * _block_diag_groups(gate_a_w[l])).astype(BF16), "ba": 0.5 * row(gate_a_b),
        "wx": (0.5 * _block_diag_groups(gate_x_w[l])).astype(BF16), "bx": 0.5 * row(gate_x_b),
        "lam": row(lru_lambda),
        "nao": row(attn_out_norm), "nlo": row(lru_out_norm), "wout": bf(w_out),
        "n2": row(ffn2_norm),
        "nf": row(final_norm),
    }

    meta = meta_tokens.astype(x.dtype)[None]
    rope_meta = _rope_table(0, N_META)
    _, _, k_meta, v_meta, u_meta, g_meta = _front(meta, rope_meta, w, N_META)[:6]
    bsz = x.shape[0]
    rep = lambda a: jnp.broadcast_to(a, (bsz,) + a.shape[1:])
    zero_h = jnp.zeros((bsz, LRU_WIDTH), F32)
    zero_tail = jnp.zeros(((CONV_W - 1) * bsz, LRU_WIDTH), F32)
    _, h_meta, tail_meta = _lru(rep(u_meta), rep(g_meta), zero_h, zero_tail, w, N_META, True)

    rope_main = _rope_table(N_META, SEQ)
    ffn2_f32 = (ffn2_w_gate[l], ffn2_w_up[l], ffn2_w_down[l])
    h1, q, k, v, u, g, w["wg2"], w["wu2"], w["wd2"] = _front(x, rope_main, w, TM, ffn2_f32)
    y_mla, y_lru = _attention_lru(q, k, v, k_meta, v_meta, u, g, h_meta, tail_meta, w)
    return _back(h1, y_mla, y_lru, w, TM)
```

```python
import functools
import math

import jax
import jax.numpy as jnp
import numpy as np
from jax import lax
from jax.experimental import pallas as pl
from jax.experimental.pallas import tpu as pltpu

D_MODEL = 1024
SEQ = 2048
N_META = 16
CHUNK = 64
HEADS = 4
D_NOPE = 128
D_ROPE = 64
D_QK = D_NOPE + D_ROPE
D_V = 128
KV_RANK = 256
Q_RANK = 384
ROPE_THETA = 10000.0
LRU_WIDTH = 512
LRU_BLOCKS = 8
LRU_BLOCK = 64
CONV_W = 4
C_RGLRU = 8.0
D_FF = 2816
EPS = 1e-6
NEG_INF = -1e30

LANES = 128
BF16_SUBLANES = 16
D_HEAD_PAD = 2 * LANES
Z_WIDTH = Q_RANK + KV_RANK + LANES + 2 * LRU_WIDTH
GATE_GROUP = 256

TM = 512
FF_CHUNK = 256
DOWN_ROWS = 256
TQ = 256
ATTN_HEADS_PER_STEP = 2
LRU_TB = 16
VMEM_LIMIT_BIG = 56 * 1024 * 1024
VMEM_LIMIT_SMALL = 40 * 1024 * 1024

F32 = jnp.float32
BF16 = jnp.bfloat16


def _rms(x, g):
    ms = jnp.mean(x * x, axis=-1, keepdims=True)
    return x * lax.rsqrt(ms + EPS) * g


def _dot(a, b):
    return jnp.dot(a, b, preferred_element_type=F32)


def _dot_nt(a, b):
    return lax.dot_general(a, b, (((1,), (1,)), ((), ())), preferred_element_type=F32)


def _gelu_tanh(x):
    c0 = math.sqrt(2.0 / math.pi)
    return x * (0.5 * jnp.tanh(x * (c0 + (c0 * 0.044715) * (x * x))) + 0.5)


def _swiglu(u, wg_ref, wu_ref, wd_ref, a_ref, hooks, finish):
    for c in range(D_FF // FF_CHUNK):
        sl = slice(c * FF_CHUNK, (c + 1) * FF_CHUNK)
        gate = _dot(u, wg_ref[:, sl])
        up = _dot(u, wu_ref[:, sl])
        a_ref[:, sl] = (jax.nn.silu(gate) * up).astype(BF16)
        if c in hooks:
            hooks[c]()
    tm = a_ref.shape[0]
    blk = min(tm, DOWN_ROWS)
    for r in range(tm // blk):
        rows = slice(r * blk, (r + 1) * blk)
        finish(rows, _dot(a_ref[rows, :], wd_ref[...]))


def _front_kernel(n_cast, x_ref, rope_ref, n1_ref, wg_ref, wu_ref, wd_ref, nmix_ref, win_ref,
                  nq_ref, wuq_ref, nkv_ref, wuk_ref, wuv_ref, qhn_ref, khn_ref, *rest):
    cast_in, rest = rest[:n_cast], rest[n_cast:]
    h_ref, q_ref, k_ref, v_ref, u_ref, g_ref = rest[:6]
    cast_out, (a_scr, un_scr) = rest[6:6 + n_cast], rest[6 + n_cast:]
    step = pl.program_id(0)
    last = pl.num_programs(0) - 1

    @pl.when(step == 0)
    def _():
        un_scr[...] = jnp.zeros_like(un_scr)

    st = {}

    def mix_in():
        z = _dot(un_scr[...], win_ref[...])
        o1 = Q_RANK
        o2 = o1 + KV_RANK
        o3 = o2 + LANES
        o4 = o3 + LRU_WIDTH
        st["c_q"], st["c_kv"], st["k_r"] = z[:, :o1], z[:, o1:o2], z[:, o2:o3]
        u_ref[0] = z[:, o3:o4].astype(u_ref.dtype)
        g_ref[0] = _gelu_tanh(z[:, o4:]).astype(g_ref.dtype)

    def mix_proj():
        st["q_all"] = _dot(_rms(st["c_q"], nq_ref[...]).astype(BF16), wuq_ref[...])
        ckv = _rms(st["c_kv"], nkv_ref[...]).astype(BF16)
        st["k_nope"] = _dot(ckv, wuk_ref[...])
        v_ref[0] = _dot(ckv, wuv_ref[...]).astype(v_ref.dtype)

    def mix_heads():
        cos, sin_lo, sin_hi = rope_ref[0], rope_ref[1], rope_ref[2]

        def rope(x):
            return (x * cos + pltpu.roll(x, LANES - D_ROPE // 2, 1) * sin_lo
                    + pltpu.roll(x, D_ROPE // 2, 1) * sin_hi)

        qhn = qhn_ref[...]
        khn = khn_ref[...]
        k_r = st["k_r"]
        kr_roped = rope(k_r * khn[:, LANES:])
        kr_ss = jnp.sum(k_r * k_r, axis=-1, keepdims=True)
        scale = math.log2(math.e) / math.sqrt(D_QK)
        for hd in range(HEADS):
            qh = st["q_all"][:, hd * D_HEAD_PAD:(hd + 1) * D_HEAD_PAD]
            q_rinv = lax.rsqrt(jnp.sum(qh * qh, axis=-1, keepdims=True) / D_QK + EPS)
            qn = qh * q_rinv * qhn
            q_out = jnp.concatenate([qn[:, :LANES], rope(qn[:, LANES:])], axis=1) * scale
            q_ref[0, hd] = q_out.astype(q_ref.dtype)
            kn = st["k_nope"][:, hd * D_NOPE:(hd + 1) * D_NOPE]
            k_rinv = lax.rsqrt((jnp.sum(kn * kn, axis=-1, keepdims=True) + kr_ss) / D_QK + EPS)
            k_out = jnp.concatenate([kn * k_rinv * khn[:, :LANES], kr_roped * k_rinv], axis=1)
            k_ref[0, hd] = k_out.astype(k_ref.dtype)

    def finish(rows, ffn):
        h = x_ref[0, rows, :] + 0.5 * ffn
        h_ref[0, rows, :] = h
        un_scr[rows, :] = _rms(h, nmix_ref[...]).astype(BF16)

    def cast_weights():
        for src, dst in zip(cast_in, cast_out):
            dst[...] = src[...].astype(dst.dtype)

    @pl.when(step < last)
    def _():
        mix_in()
        u = _rms(x_ref[0], n1_ref[...]).astype(BF16)
        _swiglu(u, wg_ref, wu_ref, wd_ref, a_scr, {3: mix_proj, 6: mix_heads, 8: cast_weights}, finish)

    @pl.when(step == last)
    def _():
        mix_in()
        mix_proj()
        mix_heads()


def _resident(shape):
    nd = len(shape)
    return pl.BlockSpec(shape, lambda *_: (0,) * nd, pipeline_mode=pl.Buffered(1))


def _cast_blocks(arr, nt):
    rows = arr.shape[0]
    steps = max(d for d in range(1, nt + 1) if rows % d == 0 and (rows // d) % BF16_SUBLANES == 0)
    return pl.BlockSpec((rows // steps, arr.shape[1]), lambda n: (jnp.minimum(n, steps - 1), 0))


def _front(x, rope_tab, w, tm, cast=()):
    bsz, s, _ = x.shape
    ns = s // tm
    nt = bsz * ns
    cast_in_specs = [_cast_blocks(a, nt) for a in cast]
    cast_out_specs = [_cast_blocks(a, nt) for a in cast]
    cur = lambda n: jnp.minimum(n, nt - 1)
    prv = lambda n: jnp.maximum(n - 1, 0)
    tok_cur = lambda n: (cur(n) // ns, cur(n) % ns, 0)
    tok_prv = lambda n: (prv(n) // ns, prv(n) % ns, 0)
    head_prv = lambda n: (prv(n) // ns, 0, prv(n) % ns, 0)
    in_specs = [
        pl.BlockSpec((1, tm, D_MODEL), tok_cur),
        pl.BlockSpec((3, tm, LANES), lambda n: (0, prv(n) % ns, 0)),
        _resident((1, D_MODEL)),
        _resident((D_MODEL, D_FF)), _resident((D_MODEL, D_FF)), _resident((D_FF, D_MODEL)),
        _resident((1, D_MODEL)), _resident((D_MODEL, Z_WIDTH)),
        _resident((1, Q_RANK)), _resident((Q_RANK, HEADS * D_HEAD_PAD)),
        _resident((1, KV_RANK)), _resident((KV_RANK, HEADS * D_NOPE)), _resident((KV_RANK, HEADS * D_V)),
        _resident((1, D_HEAD_PAD)), _resident((1, D_HEAD_PAD)),
    ] + cast_in_specs
    out_shape = [
        jax.ShapeDtypeStruct((bsz, s, D_MODEL), F32),
        jax.ShapeDtypeStruct((bsz, HEADS, s, D_HEAD_PAD), BF16),
        jax.ShapeDtypeStruct((bsz, HEADS, s, D_HEAD_PAD), BF16),
        jax.ShapeDtypeStruct((bsz, s, HEADS * D_V), BF16),
        jax.ShapeDtypeStruct((bsz, s, LRU_WIDTH), BF16),
        jax.ShapeDtypeStruct((bsz, s, LRU_WIDTH), BF16),
    ] + [jax.ShapeDtypeStruct(a.shape, BF16) for a in cast]
    out_specs = [
        pl.BlockSpec((1, tm, D_MODEL), tok_cur),
        pl.BlockSpec((1, HEADS, tm, D_HEAD_PAD), head_prv),
        pl.BlockSpec((1, HEADS, tm, D_HEAD_PAD), head_prv),
        pl.BlockSpec((1, tm, HEADS * D_V), tok_prv),
        pl.BlockSpec((1, tm, LRU_WIDTH), tok_prv),
        pl.BlockSpec((1, tm, LRU_WIDTH), tok_prv),
    ] + cast_out_specs
    return pl.pallas_call(
        functools.partial(_front_kernel, len(cast)),
        grid=(nt + 1,), in_specs=in_specs, out_specs=out_specs, out_shape=out_shape,
        scratch_shapes=[pltpu.VMEM((tm, D_FF), BF16), pltpu.VMEM((tm, D_MODEL), BF16)],
        compiler_params=pltpu.CompilerParams(
            dimension_semantics=("arbitrary",), vmem_limit_bytes=VMEM_LIMIT_BIG),
        name="front",
    )(x, rope_tab, w["n1"], w["wg1"], w["wu1"], w["wd1"], w["nmix"], w["win"],
      w["nq"], w["wuq"], w["nkv"], w["wuk"], w["wuv"], w["qhn"], w["khn"], *cast)


def _attn_body(q_ref, k_ref, v_ref, km_ref, vm_ref, o_ref, kx_scr, vx_scr, hooks):
    hp, seq = q_ref.shape[1], q_ref.shape[2]
    n_tiles = seq // TQ + 1
    end = N_META + seq
    for hd in range(hp):
        vcol = slice(hd * D_V, (hd + 1) * D_V)
        kx_scr[hd, :N_META, :] = km_ref[0, hd]
        kx_scr[hd, N_META:end, :] = k_ref[0, hd]
        vx_scr[hd, :N_META, :D_V] = vm_ref[0, :, vcol]
        vx_scr[hd, N_META:end, :D_V] = v_ref[0, :, vcol]
        vx_scr[hd, end:, :D_V] = jnp.zeros((n_tiles * TQ - end, D_V), BF16)
        kx_scr[hd, end:, :] = jnp.zeros((n_tiles * TQ - end, D_HEAD_PAD), BF16)
        vx_scr[hd, :, D_V:] = jnp.ones((n_tiles * TQ, D_V), BF16)

    def rows_of(t):
        return max(0, TQ * t - CHUNK), min(seq, TQ * (t + 1) - CHUNK)

    def staircase(t):
        lo, hi = rows_of(t)
        qchunk = (lo + lax.broadcasted_iota(jnp.int32, (hi - lo, TQ), 0)) // CHUNK
        kpos = TQ * t + lax.broadcasted_iota(jnp.int32, (hi - lo, TQ), 1)
        return kpos < N_META + CHUNK * (qchunk + 1)

    def scores(t, hd):
        lo, hi = rows_of(t)
        q = q_ref[0, hd, lo:hi, :]
        s_last = jnp.where(staircase(t), _dot_nt(q, kx_scr[hd, TQ * t:TQ * (t + 1), :]), NEG_INF)
        s_past = _dot_nt(q, kx_scr[hd, :TQ * t, :]) if t > 0 else None
        return s_last, s_past

    def finish(t, hd, s_last, s_past):
        lo, hi = rows_of(t)
        m = jnp.max(s_last, axis=-1, keepdims=True)
        if s_past is not None:
            m = jnp.maximum(m, jnp.max(s_past, axis=-1, keepdims=True))
        acc = _dot(jnp.exp2(s_last - m).astype(BF16), vx_scr[hd, TQ * t:TQ * (t + 1), :])
        if s_past is not None:
            acc = acc + _dot(jnp.exp2(s_past - m).astype(BF16), vx_scr[hd, :TQ * t, :])
        o_ref[0, lo:hi, hd * D_V:(hd + 1) * D_V] = (acc[:, :D_V] / acc[:, D_V:]).astype(o_ref.dtype)

    order = [t for pair in zip(range(n_tiles - 1, -1, -1), range(n_tiles)) for t in pair][:n_tiles]
    items = [(t, hd) for t in order for hd in range(hp)]
    s_next = scores(*items[0])
    for n, item in enumerate(items):
        s_cur = s_next
        s_next = scores(*items[n + 1]) if n + 1 < len(items) else None
        finish(*item, *s_cur)
        for hook in hooks.get(n, ()):
            hook()


def _lru_blocks(first, step0, u_ref, g_ref, perm_ref, cw_ref, cb_ref, wa_ref, ba_ref, wx_ref, bx_ref,
                lam_ref, y_ref, ubuf, hstate):
    bsz, tt, _ = u_ref.shape
    rows = LRU_TB * bsz
    tail = (CONV_W - 1) * bsz

    def gate_tanh(xcb, w_ref, b_ref):
        parts = [_dot(xcb[:, m * GATE_GROUP:(m + 1) * GATE_GROUP], w_ref[m])
                 for m in range(LRU_WIDTH // GATE_GROUP)]
        return jnp.tanh(jnp.concatenate(parts, axis=1) + b_ref[...])

    def phases(k):
        st = {}
        t0 = k * LRU_TB

        def conv():
            perm = perm_ref[...]
            uk = jnp.concatenate([u_ref[b, t0:t0 + LRU_TB, :] for b in range(bsz)], axis=0)
            ubuf[tail:tail + rows] = _dot(perm, uk)
            xc = cb_ref[...]
            for kk in range(CONV_W):
                xc = xc + cw_ref[kk:kk + 1, :] * ubuf[kk * bsz:kk * bsz + rows, :]
            ubuf[0:tail] = ubuf[rows:rows + tail]
            st["xc"] = xc
            st["xcb"] = xc.astype(BF16)

        def decay_gate():
            neg_lam = -lam_ref[...]
            softplus = jnp.maximum(neg_lam, 0.0) + jnp.log1p(jnp.exp(-jnp.abs(neg_lam)))
            half = (0.5 * C_RGLRU) * softplus
            neg_log_a = half * gate_tanh(st["xcb"], wa_ref, ba_ref) + half
            a = jnp.exp2(neg_log_a * (-math.log2(math.e)))
            m2 = jnp.tanh(neg_log_a) * (a * a + 1.0)
            mult = jnp.where(m2 > 0.0, m2 * lax.rsqrt(m2), 0.0)
            if first and k == 0:
                row = lax.broadcasted_iota(jnp.int32, (rows, LRU_WIDTH), 0)
                mult = jnp.where((row < bsz) & step0, 1.0, mult)
            st["a"], st["mult"] = a, mult

        def recurrence():
            a = st["a"]
            i_gate = 0.5 * gate_tanh(st["xcb"], wx_ref, bx_ref) + 0.5
            bt = (st["mult"] * st["xc"]) * i_gate
            h = hstate[...]
            hs = []
            for t in range(LRU_TB):
                sl = slice(t * bsz, (t + 1) * bsz)
                h = a[sl] * h + bt[sl]
                hs.append(h)
            hstate[...] = h
            st["h"] = jnp.concatenate(hs, axis=0)

        def output():
            hn = _dot(perm_ref[...], st["h"].astype(BF16))
            for b in range(bsz):
                gelu_gate = g_ref[b, t0:t0 + LRU_TB, :].astype(F32)
                y_ref[b, t0:t0 + LRU_TB, :] = (hn[b * LRU_TB:(b + 1) * LRU_TB] * gelu_gate).astype(y_ref.dtype)

        return [conv, decay_gate, recurrence, output]

    return [ph for k in range(tt // LRU_TB) for ph in phases(k)]


def _lru_kernel(first, u_ref, g_ref, hin_ref, tin_ref, perm_ref, cw_ref, cb_ref, wa_ref, ba_ref,
                wx_ref, bx_ref, lam_ref, y_ref, hout_ref, tout_ref, ubuf, hstate):
    step0 = pl.program_id(0) == 0

    @pl.when(step0)
    def _():
        ubuf[0:tin_ref.shape[0]] = tin_ref[...]
        hstate[...] = hin_ref[...]

    for blk in _lru_blocks(first, step0, u_ref, g_ref, perm_ref, cw_ref, cb_ref, wa_ref, ba_ref,
                           wx_ref, bx_ref, lam_ref, y_ref, ubuf, hstate):
        blk()
    hout_ref[...] = hstate[...]
    tout_ref[...] = ubuf[0:tout_ref.shape[0]]


def _attn_lru_kernel(q_ref, k_ref, v_ref, km_ref, vm_ref, u_ref, g_ref, hin_ref, tin_ref, perm_ref,
                     cw_ref, cb_ref, wa_ref, ba_ref, wx_ref, bx_ref, lam_ref, o_ref, y_ref,
                     kx_scr, vx_scr, ubuf, hstate):
    step0 = (pl.program_id(0) == 0) & (pl.program_id(1) == 0)

    @pl.when(step0)
    def _():
        ubuf[0:tin_ref.shape[0]] = tin_ref[...]
        hstate[...] = hin_ref[...]

    stages = _lru_blocks(False, step0, u_ref, g_ref, perm_ref, cw_ref, cb_ref, wa_ref, ba_ref,
                         wx_ref, bx_ref, lam_ref, y_ref, ubuf, hstate)
    n_items = (q_ref.shape[2] // TQ + 1) * q_ref.shape[1]
    assert len(stages) <= n_items
    hooks = {n: [stage] for n, stage in enumerate(stages)}
    _attn_body(q_ref, k_ref, v_ref, km_ref, vm_ref, o_ref, kx_scr, vx_scr, hooks)


def _lru_specs(bsz, tt, tok, const2, const3):
    rows = LRU_TB * bsz
    tail = (CONV_W - 1) * bsz
    ngrp = LRU_WIDTH // GATE_GROUP
    return [
        pl.BlockSpec((bsz, tt, LRU_WIDTH), tok),
        pl.BlockSpec((bsz, tt, LRU_WIDTH), tok),
        pl.BlockSpec((bsz, LRU_WIDTH), const2),
        pl.BlockSpec((tail, LRU_WIDTH), const2),
        pl.BlockSpec((rows, rows), const2),
        pl.BlockSpec((CONV_W, LRU_WIDTH), const2),
        pl.BlockSpec((1, LRU_WIDTH), const2),
        pl.BlockSpec((ngrp, GATE_GROUP, GATE_GROUP), const3),
        pl.BlockSpec((1, LRU_WIDTH), const2),
        pl.BlockSpec((ngrp, GATE_GROUP, GATE_GROUP), const3),
        pl.BlockSpec((1, LRU_WIDTH), const2),
        pl.BlockSpec((1, LRU_WIDTH), const2),
    ]


def _lru_operands(u, g, h_in, tail_in, w):
    bsz = u.shape[0]
    assert LRU_TB == bsz
    rows = LRU_TB * bsz
    eye = jnp.eye(rows, dtype=BF16).reshape(bsz, LRU_TB, rows)
    perm = eye.transpose(1, 0, 2).reshape(rows, rows)
    return (u, g, h_in, tail_in, perm, w["cw"], w["cb"], w["wa"], w["ba"], w["wx"], w["bx"], w["lam"])


def _lru_scratch(bsz):
    rows = LRU_TB * bsz
    tail = (CONV_W - 1) * bsz
    return [pltpu.VMEM((tail + rows, LRU_WIDTH), F32), pltpu.VMEM((bsz, LRU_WIDTH), F32)]


def _lru(u, g, h_in, tail_in, w, tt, first):
    bsz, s, _ = u.shape
    tail = (CONV_W - 1) * bsz
    tok = lambda i: (0, i, 0)
    const2 = lambda i: (0, 0)
    const3 = lambda i: (0, 0, 0)
    out_shape = [
        jax.ShapeDtypeStruct((bsz, s, LRU_WIDTH), BF16),
        jax.ShapeDtypeStruct((bsz, LRU_WIDTH), F32),
        jax.ShapeDtypeStruct((tail, LRU_WIDTH), F32),
    ]
    out_specs = [
        pl.BlockSpec((bsz, tt, LRU_WIDTH), tok),
        pl.BlockSpec((bsz, LRU_WIDTH), const2),
        pl.BlockSpec((tail, LRU_WIDTH), const2),
    ]
    return pl.pallas_call(
        functools.partial(_lru_kernel, first),
        grid=(s // tt,), in_specs=_lru_specs(bsz, tt, tok, const2, const3),
        out_specs=out_specs, out_shape=out_shape, scratch_shapes=_lru_scratch(bsz),
        compiler_params=pltpu.CompilerParams(
            dimension_semantics=("arbitrary",), vmem_limit_bytes=VMEM_LIMIT_SMALL),
        name="lru_meta",
    )(*_lru_operands(u, g, h_in, tail_in, w))


def _attention_lru(q, k, v, k_meta, v_meta, u, g, h_in, tail_in, w):
    bsz, _, s, _ = q.shape
    hp = ATTN_HEADS_PER_STEP
    ngroups = HEADS // hp
    tt = s // (bsz * ngroups)
    assert tt % LRU_TB == 0
    in_specs = [
        pl.BlockSpec((1, hp, s, D_HEAD_PAD), lambda b, h: (b, h, 0, 0)),
        pl.BlockSpec((1, hp, s, D_HEAD_PAD), lambda b, h: (b, h, 0, 0)),
        pl.BlockSpec((1, s, hp * D_V), lambda b, h: (b, 0, h)),
        pl.BlockSpec((1, hp, N_META, D_HEAD_PAD), lambda b, h: (0, h, 0, 0)),
        pl.BlockSpec((1, N_META, hp * D_V), lambda b, h: (0, 0, h)),
    ] + _lru_specs(bsz, tt, lambda b, h: (0, b * ngroups + h, 0), lambda b, h: (0, 0),
                   lambda b, h: (0, 0, 0))
    out_specs = [
        pl.BlockSpec((1, s, hp * D_V), lambda b, h: (b, 0, h)),
        pl.BlockSpec((bsz, tt, LRU_WIDTH), lambda b, h: (0, b * ngroups + h, 0)),
    ]
    out_shape = [
        jax.ShapeDtypeStruct((bsz, s, HEADS * D_V), BF16),
        jax.ShapeDtypeStruct((bsz, s, LRU_WIDTH), BF16),
    ]
    return pl.pallas_call(
        _attn_lru_kernel,
        grid=(bsz, ngroups), in_specs=in_specs, out_specs=out_specs, out_shape=out_shape,
        scratch_shapes=[pltpu.VMEM((hp, s + TQ, D_HEAD_PAD), BF16), pltpu.VMEM((hp, s + TQ, 2 * D_V), BF16)]
        + _lru_scratch(bsz),
        compiler_params=pltpu.CompilerParams(
            dimension_semantics=("arbitrary", "arbitrary"),
            vmem_limit_bytes=VMEM_LIMIT_SMALL),
        name="attn_lru",
    )(q, k, v, k_meta, v_meta, *_lru_operands(u, g, h_in, tail_in, w))


def _back_kernel(h_ref, ym_ref, yl_ref, nao_ref, nlo_ref, wout_ref, n2_ref, wg_ref, wu_ref, wd_ref,
                 nf_ref, o_ref, a_scr, u2_scr, h2_scr):
    @pl.when(pl.program_id(0) == 0)
    def _():
        u2_scr[...] = jnp.zeros_like(u2_scr)
        h2_scr[...] = jnp.zeros_like(h2_scr)

    st = {}

    def mix_out():
        ym = _rms(ym_ref[0].astype(F32), nao_ref[...]).astype(BF16)
        yl = _rms(yl_ref[0].astype(F32), nlo_ref[...]).astype(BF16)
        half = HEADS * D_V
        h2 = h_ref[0] + _dot(ym, wout_ref[:half, :]) + _dot(yl, wout_ref[half:, :])
        st["h2"] = h2
        st["u2"] = _rms(h2, n2_ref[...]).astype(BF16)

    def finish(rows, ffn):
        o_ref[0, rows, :] = _rms(h2_scr[rows, :] + 0.5 * ffn, nf_ref[...])

    _swiglu(u2_scr[...], wg_ref, wu_ref, wd_ref, a_scr, {4: mix_out}, finish)
    u2_scr[...] = st["u2"]
    h2_scr[...] = st["h2"]


def _back(h1, y_mla, y_lru, w, tm):
    bsz, s, _ = h1.shape
    ns = s // tm
    nt = bsz * ns
    cur = lambda n: jnp.minimum(n, nt - 1)
    prv = lambda n: jnp.maximum(n - 1, 0)
    tok_cur = lambda n: (cur(n) // ns, cur(n) % ns, 0)
    tok_prv = lambda n: (prv(n) // ns, prv(n) % ns, 0)
    in_specs = [
        pl.BlockSpec((1, tm, D_MODEL), tok_cur),
        pl.BlockSpec((1, tm, HEADS * D_V), tok_cur),
        pl.BlockSpec((1, tm, LRU_WIDTH), tok_cur),
        _resident((1, HEADS * D_V)), _resident((1, LRU_WIDTH)),
        _resident((HEADS * D_V + LRU_WIDTH, D_MODEL)),
        _resident((1, D_MODEL)),
        _resident((D_MODEL, D_FF)), _resident((D_MODEL, D_FF)), _resident((D_FF, D_MODEL)),
        _resident((1, D_MODEL)),
    ]
    return pl.pallas_call(
        _back_kernel,
        grid=(nt + 1,), in_specs=in_specs,
        out_specs=pl.BlockSpec((1, tm, D_MODEL), tok_prv),
        out_shape=jax.ShapeDtypeStruct((bsz, s, D_MODEL), F32),
        scratch_shapes=[pltpu.VMEM((tm, D_FF), BF16), pltpu.VMEM((tm, D_MODEL), BF16),
                        pltpu.VMEM((tm, D_MODEL), F32)],
        compiler_params=pltpu.CompilerParams(
            dimension_semantics=("arbitrary",), vmem_limit_bytes=VMEM_LIMIT_BIG),
        name="back",
    )(h1, y_mla, y_lru, w["nao"], w["nlo"], w["wout"], w["n2"], w["wg2"], w["wu2"], w["wd2"], w["nf"])


def _rope_table(first_pos, n):
    half = D_ROPE // 2
    inv_freq = np.float32(ROPE_THETA) ** (-np.arange(0, half, dtype=np.float32) / np.float32(half))
    ang = np.arange(first_pos, first_pos + n, dtype=np.float32)[:, None] * inv_freq[None, :]
    cos, sin = np.cos(ang), np.sin(ang)
    zeros = np.zeros_like(cos)
    pad = np.zeros((n, LANES - D_ROPE), np.float32)
    return jnp.asarray(np.stack([
        np.concatenate([cos, cos, pad], axis=1),
        np.concatenate([-sin, zeros, pad], axis=1),
        np.concatenate([zeros, sin, pad], axis=1),
    ]).astype(np.float32))


def _block_diag_groups(wgt):
    per = GATE_GROUP // LRU_BLOCK
    w4 = wgt.reshape(LRU_BLOCKS // per, per, LRU_BLOCK, LRU_BLOCK)
    eye = jnp.eye(per, dtype=wgt.dtype)
    return jnp.einsum('mnij,nk->mnikj', w4, eye).reshape(LRU_BLOCKS // per, GATE_GROUP, GATE_GROUP)


def kernel(x, meta_tokens, ffn1_norm, ffn1_w_gate, ffn1_w_up, ffn1_w_down, mix_norm, w_in, q_latent_norm, w_uq, kv_latent_norm, w_uk, w_uv, q_head_norm, k_head_norm, conv_w, conv_b, gate_a_w, gate_a_b, gate_x_w, gate_x_b, lru_lambda, attn_out_norm, lru_out_norm, w_out, ffn2_norm, ffn2_w_gate, ffn2_w_up, ffn2_w_down, final_norm):
    l = 0
    row = lambda a: a[l].reshape(1, -1).astype(F32)
    bf = lambda a: a[l].astype(BF16)

    o2 = Q_RANK + KV_RANK
    o3 = o2 + D_ROPE
    win = w_in[l].astype(BF16)
    win_p = jnp.concatenate(
        [win[:, :o3], jnp.zeros((D_MODEL, LANES - D_ROPE), win.dtype), win[:, o3:]], axis=1)
    wuq_p = jnp.pad(w_uq[l].reshape(Q_RANK, HEADS, D_QK),
                    ((0, 0), (0, 0), (0, D_HEAD_PAD - D_QK))).reshape(Q_RANK, HEADS * D_HEAD_PAD)
    pad_head = lambda a: jnp.pad(a[l].astype(F32), (0, D_HEAD_PAD - D_QK)).reshape(1, D_HEAD_PAD)
    w = {
        "n1": row(ffn1_norm), "wg1": bf(ffn1_w_gate), "wu1": bf(ffn1_w_up), "wd1": bf(ffn1_w_down),
        "nmix": row(mix_norm), "win": win_p.astype(BF16),
        "nq": row(q_latent_norm), "wuq": wuq_p.astype(BF16),
        "nkv": row(kv_latent_norm), "wuk": bf(w_uk), "wuv": bf(w_uv),
        "qhn": pad_head(q_head_norm), "khn": pad_head(k_head_norm),
        "cw": conv_w[l].astype(F32), "cb": row(conv_b),
        "wa": (0.5 * _block_diag_groups(gate_a_w[l])).astype(BF16), "ba": 0.5 * row(gate_a_b),
        "wx": (0.5 * _block_diag_groups(gate_x_w[l])).astype(BF16), "bx": 0.5 * row(gate_x_b),
        "lam": row(lru_lambda),
        "nao": row(attn_out_norm), "nlo": row(lru_out_norm), "wout": bf(w_out),
        "n2": row(ffn2_norm),
        "nf": row(final_norm),
    }

    meta = meta_tokens.astype(x.dtype)[None]
    rope_meta = _rope_table(0, N_META)
    _, _, k_meta, v_meta, u_meta, g_meta = _front(meta, rope_meta, w, N_META)[:6]
    bsz = x.shape[0]
    rep = lambda a: jnp.broadcast_to(a, (bsz,) + a.shape[1:])
    zero_h = jnp.zeros((bsz, LRU_WIDTH), F32)
    zero_tail = jnp.zeros(((CONV_W - 1) * bsz, LRU_WIDTH), F32)
    _, h_meta, tail_meta = _lru(rep(u_meta), rep(g_meta), zero_h, zero_tail, w, N_META, True)

    rope_main = _rope_table(N_META, SEQ)
    ffn2_f32 = (ffn2_w_gate[l], ffn2_w_up[l], ffn2_w_down[l])
    h1, q, k, v, u, g, w["wg2"], w["wu2"], w["wd2"] = _front(x, rope_main, w, TM, ffn2_f32)
    y_mla, y_lru = _attention_lru(q, k, v, k_meta, v_meta, u, g, h_meta, tail_meta, w)
    return _back(h1, y_mla, y_lru, w, TM)
```

```python
import functools
import math

import jax
import jax.numpy as jnp
import numpy as np
from jax import lax
from jax.experimental import pallas as pl
from jax.experimental.pallas import tpu as pltpu

D_MODEL = 1024
SEQ = 2048
N_META = 16
CHUNK = 64
HEADS = 4
D_NOPE = 128
D_ROPE = 64
D_QK = D_NOPE + D_ROPE
D_V = 128
KV_RANK = 256
Q_RANK = 384
ROPE_THETA = 10000.0
LRU_WIDTH = 512
LRU_BLOCKS = 8
LRU_BLOCK = 64
CONV_W = 4
C_RGLRU = 8.0
D_FF = 2816
EPS = 1e-6
NEG_INF = -1e30

LANES = 128
BF16_SUBLANES = 16
D_HEAD_PAD = 2 * LANES
Z_WIDTH = Q_RANK + KV_RANK + D_ROPE + 2 * LRU_WIDTH
GATE_GROUP = 256

TM = 512
FF_CHUNK = 256
DOWN_ROWS = 256
TQ = 256
ATTN_HEADS_PER_STEP = 2
LRU_TB = 16
VMEM_LIMIT_BIG = 56 * 1024 * 1024
VMEM_LIMIT_SMALL = 40 * 1024 * 1024

F32 = jnp.float32
BF16 = jnp.bfloat16


def _rms(x, g):
    ms = jnp.mean(x * x, axis=-1, keepdims=True)
    return x * lax.rsqrt(ms + EPS) * g


def _dot(a, b):
    return jnp.dot(a, b, preferred_element_type=F32)


def _dot_nt(a, b):
    return lax.dot_general(a, b, (((1,), (1,)), ((), ())), preferred_element_type=F32)


def _gelu_tanh(x):
    c0 = math.sqrt(2.0 / math.pi)
    return x * (0.5 * jnp.tanh(x * (c0 + (c0 * 0.044715) * (x * x))) + 0.5)


def _swiglu(u, wg_ref, wu_ref, wd_ref, a_ref, hooks, finish):
    for c in range(D_FF // FF_CHUNK):
        sl = slice(c * FF_CHUNK, (c + 1) * FF_CHUNK)
        gate = _dot(u, wg_ref[:, sl])
        up = _dot(u, wu_ref[:, sl])
        a_ref[:, sl] = (jax.nn.silu(gate) * up).astype(BF16)
        if c in hooks:
            hooks[c]()
    tm = a_ref.shape[0]
    blk = min(tm, DOWN_ROWS)
    for r in range(tm // blk):
        rows = slice(r * blk, (r + 1) * blk)
        finish(rows, _dot(a_ref[rows, :], wd_ref[...]))


def _front_kernel(n_cast, x_ref, rope_ref, n1_ref, wg_ref, wu_ref, wd_ref, nmix_ref, win_ref,
                  nq_ref, wuq_ref, nkv_ref, wuk_ref, wuv_ref, qhn_ref, khn_ref, *rest):
    cast_in, rest = rest[:n_cast], rest[n_cast:]
    h_ref, q_ref, k_ref, v_ref, u_ref, g_ref = rest[:6]
    cast_out, (a_scr, un_scr) = rest[6:6 + n_cast], rest[6 + n_cast:]
    step = pl.program_id(0)
    last = pl.num_programs(0) - 1

    @pl.when(step == 0)
    def _():
        un_scr[...] = jnp.zeros_like(un_scr)

    st = {}

    def mix_in():
        z = _dot(un_scr[...], win_ref[...])
        o1 = Q_RANK
        o2 = o1 + KV_RANK
        o3 = o2 + D_ROPE
        o4 = o3 + LRU_WIDTH
        lane = lax.broadcasted_iota(jnp.int32, (z.shape[0], LANES), 1)
        st["c_q"], st["c_kv"] = z[:, :o1], z[:, o1:o2]
        st["k_r"] = jnp.where(lane < D_ROPE, z[:, o2:o2 + LANES], 0.0)
        u_ref[0] = z[:, o3:o4].astype(u_ref.dtype)
        g_ref[0] = _gelu_tanh(z[:, o4:]).astype(g_ref.dtype)

    def mix_proj():
        st["q_all"] = _dot(_rms(st["c_q"], nq_ref[...]).astype(BF16), wuq_ref[...])
        ckv = _rms(st["c_kv"], nkv_ref[...]).astype(BF16)
        st["k_nope"] = _dot(ckv, wuk_ref[...])
        v_ref[0] = _dot(ckv, wuv_ref[...]).astype(v_ref.dtype)

    def mix_heads():
        cos, sin_lo, sin_hi = rope_ref[0], rope_ref[1], rope_ref[2]

        def rope(x):
            return (x * cos + pltpu.roll(x, LANES - D_ROPE // 2, 1) * sin_lo
                    + pltpu.roll(x, D_ROPE // 2, 1) * sin_hi)

        qhn = qhn_ref[...]
        khn = khn_ref[...]
        k_r = st["k_r"]
        kr_roped = rope(k_r * khn[:, LANES:])
        kr_ss = jnp.sum(k_r * k_r, axis=-1, keepdims=True)
        scale = math.log2(math.e) / math.sqrt(D_QK)
        for hd in range(HEADS):
            qh = st["q_all"][:, hd * D_HEAD_PAD:(hd + 1) * D_HEAD_PAD]
            q_rinv = lax.rsqrt(jnp.sum(qh * qh, axis=-1, keepdims=True) / D_QK + EPS)
            qn = qh * q_rinv * qhn
            q_out = jnp.concatenate([qn[:, :LANES], rope(qn[:, LANES:])], axis=1) * scale
            q_ref[0, hd] = q_out.astype(q_ref.dtype)
            kn = st["k_nope"][:, hd * D_NOPE:(hd + 1) * D_NOPE]
            k_rinv = lax.rsqrt((jnp.sum(kn * kn, axis=-1, keepdims=True) + kr_ss) / D_QK + EPS)
            k_out = jnp.concatenate([kn * k_rinv * khn[:, :LANES], kr_roped * k_rinv], axis=1)
            k_ref[0, hd] = k_out.astype(k_ref.dtype)

    def finish(rows, ffn):
        h = x_ref[0, rows, :] + 0.5 * ffn
        h_ref[0, rows, :] = h
        un_scr[rows, :] = _rms(h, nmix_ref[...]).astype(BF16)

    def cast_weights():
        for src, dst in zip(cast_in, cast_out):
            dst[...] = src[...].astype(dst.dtype)

    @pl.when(step < last)
    def _():
        mix_in()
        u = _rms(x_ref[0], n1_ref[...]).astype(BF16)
        _swiglu(u, wg_ref, wu_ref, wd_ref, a_scr, {3: mix_proj, 6: mix_heads, 8: cast_weights}, finish)

    @pl.when(step == last)
    def _():
        mix_in()
        mix_proj()
        mix_heads()


def _resident(shape):
    nd = len(shape)
    return pl.BlockSpec(shape, lambda *_: (0,) * nd, pipeline_mode=pl.Buffered(1))


def _cast_blocks(arr, nt):
    rows = arr.shape[0]
    steps = max(d for d in range(1, nt + 1) if rows % d == 0 and (rows // d) % BF16_SUBLANES == 0)
    return pl.BlockSpec((rows // steps, arr.shape[1]), lambda n: (jnp.minimum(n, steps - 1), 0))


def _front(x, rope_tab, w, tm, cast=()):
    bsz, s, _ = x.shape
    ns = s // tm
    nt = bsz * ns
    cast_in_specs = [_cast_blocks(a, nt) for a in cast]
    cast_out_specs = [_cast_blocks(a, nt) for a in cast]
    cur = lambda n: jnp.minimum(n, nt - 1)
    prv = lambda n: jnp.maximum(n - 1, 0)
    tok_cur = lambda n: (cur(n) // ns, cur(n) % ns, 0)
    tok_prv = lambda n: (prv(n) // ns, prv(n) % ns, 0)
    head_prv = lambda n: (prv(n) // ns, 0, prv(n) % ns, 0)
    in_specs = [
        pl.BlockSpec((1, tm, D_MODEL), tok_cur),
        pl.BlockSpec((3, tm, LANES), lambda n: (0, prv(n) % ns, 0)),
        _resident((1, D_MODEL)),
        _resident((D_MODEL, D_FF)), _resident((D_MODEL, D_FF)), _resident((D_FF, D_MODEL)),
        _resident((1, D_MODEL)), _resident((D_MODEL, Z_WIDTH)),
        _resident((1, Q_RANK)), _resident((Q_RANK, HEADS * D_HEAD_PAD)),
        _resident((1, KV_RANK)), _resident((KV_RANK, HEADS * D_NOPE)), _resident((KV_RANK, HEADS * D_V)),
        _resident((1, D_HEAD_PAD)), _resident((1, D_HEAD_PAD)),
    ] + cast_in_specs
    out_shape = [
        jax.ShapeDtypeStruct((bsz, s, D_MODEL), F32),
        jax.ShapeDtypeStruct((bsz, HEADS, s, D_HEAD_PAD), BF16),
        jax.ShapeDtypeStruct((bsz, HEADS, s, D_HEAD_PAD), BF16),
        jax.ShapeDtypeStruct((bsz, s, HEADS * D_V), BF16),
        jax.ShapeDtypeStruct((bsz, s, LRU_WIDTH), BF16),
        jax.ShapeDtypeStruct((bsz, s, LRU_WIDTH), BF16),
    ] + [jax.ShapeDtypeStruct(a.shape, BF16) for a in cast]
    out_specs = [
        pl.BlockSpec((1, tm, D_MODEL), tok_cur),
        pl.BlockSpec((1, HEADS, tm, D_HEAD_PAD), head_prv),
        pl.BlockSpec((1, HEADS, tm, D_HEAD_PAD), head_prv),
        pl.BlockSpec((1, tm, HEADS * D_V), tok_prv),
        pl.BlockSpec((1, tm, LRU_WIDTH), tok_prv),
        pl.BlockSpec((1, tm, LRU_WIDTH), tok_prv),
    ] + cast_out_specs
    return pl.pallas_call(
        functools.partial(_front_kernel, len(cast)),
        grid=(nt + 1,), in_specs=in_specs, out_specs=out_specs, out_shape=out_shape,
        scratch_shapes=[pltpu.VMEM((tm, D_FF), BF16), pltpu.VMEM((tm, D_MODEL), BF16)],
        compiler_params=pltpu.CompilerParams(
            dimension_semantics=("arbitrary",), vmem_limit_bytes=VMEM_LIMIT_BIG),
        name="front",
    )(x, rope_tab, w["n1"], w["wg1"], w["wu1"], w["wd1"], w["nmix"], w["win"],
      w["nq"], w["wuq"], w["nkv"], w["wuk"], w["wuv"], w["qhn"], w["khn"], *cast)


def _attn_body(q_ref, k_ref, v_ref, km_ref, vm_ref, o_ref, kx_scr, vx_scr, hooks):
    hp, seq = q_ref.shape[1], q_ref.shape[2]
    n_tiles = seq // TQ + 1
    end = N_META + seq
    for hd in range(hp):
        vcol = slice(hd * D_V, (hd + 1) * D_V)
        kx_scr[hd, :N_META, :] = km_ref[0, hd]
        kx_scr[hd, N_META:end, :] = k_ref[0, hd]
        vx_scr[hd, :N_META, :D_V] = vm_ref[0, :, vcol]
        vx_scr[hd, N_META:end, :D_V] = v_ref[0, :, vcol]
        vx_scr[hd, end:, :D_V] = jnp.zeros((n_tiles * TQ - end, D_V), BF16)
        kx_scr[hd, end:, :] = jnp.zeros((n_tiles * TQ - end, D_HEAD_PAD), BF16)
        vx_scr[hd, :, D_V:] = jnp.ones((n_tiles * TQ, D_V), BF16)

    def rows_of(t):
        return max(0, TQ * t - CHUNK), min(seq, TQ * (t + 1) - CHUNK)

    def staircase(t):
        lo, hi = rows_of(t)
        qchunk = (lo + lax.broadcasted_iota(jnp.int32, (hi - lo, TQ), 0)) // CHUNK
        kpos = TQ * t + lax.broadcasted_iota(jnp.int32, (hi - lo, TQ), 1)
        return kpos < N_META + CHUNK * (qchunk + 1)

    def scores(t, hd):
        lo, hi = rows_of(t)
        q = q_ref[0, hd, lo:hi, :]
        s_last = jnp.where(staircase(t), _dot_nt(q, kx_scr[hd, TQ * t:TQ * (t + 1), :]), NEG_INF)
        s_past = _dot_nt(q, kx_scr[hd, :TQ * t, :]) if t > 0 else None
        return s_last, s_past

    def finish(t, hd, s_last, s_past):
        lo, hi = rows_of(t)
        m = jnp.max(s_last, axis=-1, keepdims=True)
        if s_past is not None:
            m = jnp.maximum(m, jnp.max(s_past, axis=-1, keepdims=True))
        acc = _dot(jnp.exp2(s_last - m).astype(BF16), vx_scr[hd, TQ * t:TQ * (t + 1), :])
        if s_past is not None:
            acc = acc + _dot(jnp.exp2(s_past - m).astype(BF16), vx_scr[hd, :TQ * t, :])
        o_ref[0, lo:hi, hd * D_V:(hd + 1) * D_V] = (acc[:, :D_V] / acc[:, D_V:]).astype(o_ref.dtype)

    order = [t for pair in zip(range(n_tiles - 1, -1, -1), range(n_tiles)) for t in pair][:n_tiles]
    items = [(t, hd) for t in order for hd in range(hp)]
    s_next = scores(*items[0])
    for n, item in enumerate(items):
        s_cur = s_next
        s_next = scores(*items[n + 1]) if n + 1 < len(items) else None
        finish(*item, *s_cur)
        for hook in hooks.get(n, ()):
            hook()


def _lru_blocks(first, step0, u_ref, perm_ref, cw_ref, cb_ref, wa_ref, ba_ref, wx_ref, bx_ref,
                lam_ref, y_ref, ubuf, hstate):
    bsz, tt, _ = u_ref.shape
    rows = LRU_TB * bsz
    tail = (CONV_W - 1) * bsz

    def gate_tanh(xcb, w_ref, b_ref):
        parts = [_dot(xcb[:, m * GATE_GROUP:(m + 1) * GATE_GROUP], w_ref[m])
                 for m in range(LRU_WIDTH // GATE_GROUP)]
        return jnp.tanh(jnp.concatenate(parts, axis=1) + b_ref[...])

    def phases(k):
        st = {}
        t0 = k * LRU_TB

        def conv():
            perm = perm_ref[...]
            uk = jnp.concatenate([u_ref[b, t0:t0 + LRU_TB, :] for b in range(bsz)], axis=0)
            ubuf[tail:tail + rows] = _dot(perm, uk)
            xc = cb_ref[...]
            for kk in range(CONV_W):
                xc = xc + cw_ref[kk:kk + 1, :] * ubuf[kk * bsz:kk * bsz + rows, :]
            ubuf[0:tail] = ubuf[rows:rows + tail]
            st["xc"] = xc
            st["xcb"] = xc.astype(BF16)

        def decay_gate():
            neg_lam = -lam_ref[...]
            softplus = jnp.maximum(neg_lam, 0.0) + jnp.log1p(jnp.exp(-jnp.abs(neg_lam)))
            half = (0.5 * C_RGLRU) * softplus
            neg_log_a = half * gate_tanh(st["xcb"], wa_ref, ba_ref) + half
            a = jnp.exp2(neg_log_a * (-math.log2(math.e)))
            m2 = jnp.tanh(neg_log_a) * (a * a + 1.0)
            mult = jnp.where(m2 > 0.0, m2 * lax.rsqrt(m2), 0.0)
            if first and k == 0:
                row = lax.broadcasted_iota(jnp.int32, (rows, LRU_WIDTH), 0)
                mult = jnp.where((row < bsz) & step0, 1.0, mult)
            st["a"], st["mult"] = a, mult

        def recurrence():
            a = st["a"]
            i_gate = 0.5 * gate_tanh(st["xcb"], wx_ref, bx_ref) + 0.5
            bt = (st["mult"] * st["xc"]) * i_gate
            h = hstate[...]
            hs = []
            for t in range(LRU_TB):
                sl = slice(t * bsz, (t + 1) * bsz)
                h = a[sl] * h + bt[sl]
                hs.append(h)
            hstate[...] = h
            st["h"] = jnp.concatenate(hs, axis=0)

        def output():
            hn = _dot(perm_ref[...], st["h"].astype(BF16))
            for b in range(bsz):
                y_ref[b, t0:t0 + LRU_TB, :] = hn[b * LRU_TB:(b + 1) * LRU_TB].astype(y_ref.dtype)

        return [conv, decay_gate, recurrence, output]

    return [ph for k in range(tt // LRU_TB) for ph in phases(k)]


def _lru_kernel(first, u_ref, hin_ref, tin_ref, perm_ref, cw_ref, cb_ref, wa_ref, ba_ref,
                wx_ref, bx_ref, lam_ref, y_ref, hout_ref, tout_ref, ubuf, hstate):
    step0 = pl.program_id(0) == 0

    @pl.when(step0)
    def _():
        ubuf[0:tin_ref.shape[0]] = tin_ref[...]
        hstate[...] = hin_ref[...]

    for blk in _lru_blocks(first, step0, u_ref, perm_ref, cw_ref, cb_ref, wa_ref, ba_ref,
                           wx_ref, bx_ref, lam_ref, y_ref, ubuf, hstate):
        blk()
    hout_ref[...] = hstate[...]
    tout_ref[...] = ubuf[0:tout_ref.shape[0]]


def _attn_lru_kernel(q_ref, k_ref, v_ref, km_ref, vm_ref, u_ref, hin_ref, tin_ref, perm_ref,
                     cw_ref, cb_ref, wa_ref, ba_ref, wx_ref, bx_ref, lam_ref, o_ref, y_ref,
                     kx_scr, vx_scr, ubuf, hstate):
    step0 = (pl.program_id(0) == 0) & (pl.program_id(1) == 0)

    @pl.when(step0)
    def _():
        ubuf[0:tin_ref.shape[0]] = tin_ref[...]
        hstate[...] = hin_ref[...]

    stages = _lru_blocks(False, step0, u_ref, perm_ref, cw_ref, cb_ref, wa_ref, ba_ref,
                         wx_ref, bx_ref, lam_ref, y_ref, ubuf, hstate)
    n_items = (q_ref.shape[2] // TQ + 1) * q_ref.shape[1]
    assert len(stages) <= n_items
    hooks = {n: [stage] for n, stage in enumerate(stages)}
    _attn_body(q_ref, k_ref, v_ref, km_ref, vm_ref, o_ref, kx_scr, vx_scr, hooks)


def _lru_specs(bsz, tt, tok, const2, const3):
    rows = LRU_TB * bsz
    tail = (CONV_W - 1) * bsz
    ngrp = LRU_WIDTH // GATE_GROUP
    return [
        pl.BlockSpec((bsz, tt, LRU_WIDTH), tok),
        pl.BlockSpec((bsz, LRU_WIDTH), const2),
        pl.BlockSpec((tail, LRU_WIDTH), const2),
        pl.BlockSpec((rows, rows), const2),
        pl.BlockSpec((CONV_W, LRU_WIDTH), const2),
        pl.BlockSpec((1, LRU_WIDTH), const2),
        pl.BlockSpec((ngrp, GATE_GROUP, GATE_GROUP), const3),
        pl.BlockSpec((1, LRU_WIDTH), const2),
        pl.BlockSpec((ngrp, GATE_GROUP, GATE_GROUP), const3),
        pl.BlockSpec((1, LRU_WIDTH), const2),
        pl.BlockSpec((1, LRU_WIDTH), const2),
    ]


def _lru_operands(u, h_in, tail_in, w):
    bsz = u.shape[0]
    assert LRU_TB == bsz
    rows = LRU_TB * bsz
    eye = jnp.eye(rows, dtype=BF16).reshape(bsz, LRU_TB, rows)
    perm = eye.transpose(1, 0, 2).reshape(rows, rows)
    return (u, h_in, tail_in, perm, w["cw"], w["cb"], w["wa"], w["ba"], w["wx"], w["bx"], w["lam"])


def _lru_scratch(bsz):
    rows = LRU_TB * bsz
    tail = (CONV_W - 1) * bsz
    return [pltpu.VMEM((tail + rows, LRU_WIDTH), F32), pltpu.VMEM((bsz, LRU_WIDTH), F32)]


def _lru(u, h_in, tail_in, w, tt, first):
    bsz, s, _ = u.shape
    tail = (CONV_W - 1) * bsz
    tok = lambda i: (0, i, 0)
    const2 = lambda i: (0, 0)
    const3 = lambda i: (0, 0, 0)
    out_shape = [
        jax.ShapeDtypeStruct((bsz, s, LRU_WIDTH), BF16),
        jax.ShapeDtypeStruct((bsz, LRU_WIDTH), F32),
        jax.ShapeDtypeStruct((tail, LRU_WIDTH), F32),
    ]
    out_specs = [
        pl.BlockSpec((bsz, tt, LRU_WIDTH), tok),
        pl.BlockSpec((bsz, LRU_WIDTH), const2),
        pl.BlockSpec((tail, LRU_WIDTH), const2),
    ]
    return pl.pallas_call(
        functools.partial(_lru_kernel, first),
        grid=(s // tt,), in_specs=_lru_specs(bsz, tt, tok, const2, const3),
        out_specs=out_specs, out_shape=out_shape, scratch_shapes=_lru_scratch(bsz),
        compiler_params=pltpu.CompilerParams(
            dimension_semantics=("arbitrary",), vmem_limit_bytes=VMEM_LIMIT_SMALL),
        name="lru_meta",
    )(*_lru_operands(u, h_in, tail_in, w))


def _attention_lru(q, k, v, k_meta, v_meta, u, h_in, tail_in, w):
    bsz, _, s, _ = q.shape
    hp = ATTN_HEADS_PER_STEP
    ngroups = HEADS // hp
    tt = s // (bsz * ngroups)
    assert tt % LRU_TB == 0
    in_specs = [
        pl.BlockSpec((1, hp, s, D_HEAD_PAD), lambda b, h: (b, h, 0, 0)),
        pl.BlockSpec((1, hp, s, D_HEAD_PAD), lambda b, h: (b, h, 0, 0)),
        pl.BlockSpec((1, s, hp * D_V), lambda b, h: (b, 0, h)),
        pl.BlockSpec((1, hp, N_META, D_HEAD_PAD), lambda b, h: (0, h, 0, 0)),
        pl.BlockSpec((1, N_META, hp * D_V), lambda b, h: (0, 0, h)),
    ] + _lru_specs(bsz, tt, lambda b, h: (0, b * ngroups + h, 0), lambda b, h: (0, 0),
                   lambda b, h: (0, 0, 0))
    out_specs = [
        pl.BlockSpec((1, s, hp * D_V), lambda b, h: (b, 0, h)),
        pl.BlockSpec((bsz, tt, LRU_WIDTH), lambda b, h: (0, b * ngroups + h, 0)),
    ]
    out_shape = [
        jax.ShapeDtypeStruct((bsz, s, HEADS * D_V), BF16),
        jax.ShapeDtypeStruct((bsz, s, LRU_WIDTH), BF16),
    ]
    return pl.pallas_call(
        _attn_lru_kernel,
        grid=(bsz, ngroups), in_specs=in_specs, out_specs=out_specs, out_shape=out_shape,
        scratch_shapes=[pltpu.VMEM((hp, s + TQ, D_HEAD_PAD), BF16), pltpu.VMEM((hp, s + TQ, 2 * D_V), BF16)]
        + _lru_scratch(bsz),
        compiler_params=pltpu.CompilerParams(
            dimension_semantics=("arbitrary", "arbitrary"),
            vmem_limit_bytes=VMEM_LIMIT_SMALL),
        name="attn_lru",
    )(q, k, v, k_meta, v_meta, *_lru_operands(u, h_in, tail_in, w))


def _back_kernel(h_ref, ym_ref, hl_ref, gg_ref, nao_ref, nlo_ref, wout_ref, n2_ref, wg_ref, wu_ref, wd_ref,
                 nf_ref, o_ref, a_scr, u2_scr, h2_scr):
    @pl.when(pl.program_id(0) == 0)
    def _():
        u2_scr[...] = jnp.zeros_like(u2_scr)
        h2_scr[...] = jnp.zeros_like(h2_scr)

    st = {}

    def mix_out():
        ym = _rms(ym_ref[0].astype(F32), nao_ref[...]).astype(BF16)
        yl = _rms(hl_ref[0].astype(F32) * gg_ref[0].astype(F32), nlo_ref[...]).astype(BF16)
        half = HEADS * D_V
        h2 = h_ref[0] + _dot(ym, wout_ref[:half, :]) + _dot(yl, wout_ref[half:, :])
        st["h2"] = h2
        st["u2"] = _rms(h2, n2_ref[...]).astype(BF16)

    def finish(rows, ffn):
        o_ref[0, rows, :] = _rms(h2_scr[rows, :] + 0.5 * ffn, nf_ref[...])

    _swiglu(u2_scr[...], wg_ref, wu_ref, wd_ref, a_scr, {4: mix_out}, finish)
    u2_scr[...] = st["u2"]
    h2_scr[...] = st["h2"]


def _back(h1, y_mla, h_lru, gelu_gate, w, tm):
    bsz, s, _ = h1.shape
    ns = s // tm
    nt = bsz * ns
    cur = lambda n: jnp.minimum(n, nt - 1)
    prv = lambda n: jnp.maximum(n - 1, 0)
    tok_cur = lambda n: (cur(n) // ns, cur(n) % ns, 0)
    tok_prv = lambda n: (prv(n) // ns, prv(n) % ns, 0)
    in_specs = [
        pl.BlockSpec((1, tm, D_MODEL), tok_cur),
        pl.BlockSpec((1, tm, HEADS * D_V), tok_cur),
        pl.BlockSpec((1, tm, LRU_WIDTH), tok_cur),
        pl.BlockSpec((1, tm, LRU_WIDTH), tok_cur),
        _resident((1, HEADS * D_V)), _resident((1, LRU_WIDTH)),
        _resident((HEADS * D_V + LRU_WIDTH, D_MODEL)),
        _resident((1, D_MODEL)),
        _resident((D_MODEL, D_FF)), _resident((D_MODEL, D_FF)), _resident((D_FF, D_MODEL)),
        _resident((1, D_MODEL)),
    ]
    return pl.pallas_call(
        _back_kernel,
        grid=(nt + 1,), in_specs=in_specs,
        out_specs=pl.BlockSpec((1, tm, D_MODEL), tok_prv),
        out_shape=jax.ShapeDtypeStruct((bsz, s, D_MODEL), F32),
        scratch_shapes=[pltpu.VMEM((tm, D_FF), BF16), pltpu.VMEM((tm, D_MODEL), BF16),
                        pltpu.VMEM((tm, D_MODEL), F32)],
        compiler_params=pltpu.CompilerParams(
            dimension_semantics=("arbitrary",), vmem_limit_bytes=VMEM_LIMIT_BIG),
        name="back",
    )(h1, y_mla, h_lru, gelu_gate, w["nao"], w["nlo"], w["wout"], w["n2"], w["wg2"], w["wu2"], w["wd2"],
      w["nf"])


def _rope_table(first_pos, n):
    half = D_ROPE // 2
    inv_freq = np.float32(ROPE_THETA) ** (-np.arange(0, half, dtype=np.float32) / np.float32(half))
    ang = np.arange(first_pos, first_pos + n, dtype=np.float32)[:, None] * inv_freq[None, :]
    cos, sin = np.cos(ang), np.sin(ang)
    zeros = np.zeros_like(cos)
    pad = np.zeros((n, LANES - D_ROPE), np.float32)
    return jnp.asarray(np.stack([
        np.concatenate([cos, cos, pad], axis=1),
        np.concatenate([-sin, zeros, pad], axis=1),
        np.concatenate([zeros, sin, pad], axis=1),
    ]).astype(np.float32))


def _block_diag_groups(wgt):
    per = GATE_GROUP // LRU_BLOCK
    w4 = wgt.reshape(LRU_BLOCKS // per, per, LRU_BLOCK, LRU_BLOCK)
    eye = jnp.eye(per, dtype=wgt.dtype)
    return jnp.einsum('mnij,nk->mnikj', w4, eye).reshape(LRU_BLOCKS // per, GATE_GROUP, GATE_GROUP)


def kernel(x, meta_tokens, ffn1_norm, ffn1_w_gate, ffn1_w_up, ffn1_w_down, mix_norm, w_in, q_latent_norm, w_uq, kv_latent_norm, w_uk, w_uv, q_head_norm, k_head_norm, conv_w, conv_b, gate_a_w, gate_a_b, gate_x_w, gate_x_b, lru_lambda, attn_out_norm, lru_out_norm, w_out, ffn2_norm, ffn2_w_gate, ffn2_w_up, ffn2_w_down, final_norm):
    l = 0
    row = lambda a: a[l].reshape(1, -1).astype(F32)
    bf = lambda a: a[l].astype(BF16)

    wuq_p = jnp.pad(w_uq[l].reshape(Q_RANK, HEADS, D_QK),
                    ((0, 0), (0, 0), (0, D_HEAD_PAD - D_QK))).reshape(Q_RANK, HEADS * D_HEAD_PAD)
    pad_head = lambda a: jnp.pad(a[l].astype(F32), (0, D_HEAD_PAD - D_QK)).reshape(1, D_HEAD_PAD)
    w = {
        "n1": row(ffn1_norm), "wg1": bf(ffn1_w_gate), "wu1": bf(ffn1_w_up), "wd1": bf(ffn1_w_down),
        "nmix": row(mix_norm), "win": bf(w_in),
        "nq": row(q_latent_norm), "wuq": wuq_p.astype(BF16),
        "nkv": row(kv_latent_norm), "wuk": bf(w_uk), "wuv": bf(w_uv),
        "qhn": pad_head(q_head_norm), "khn": pad_head(k_head_norm),
        "cw": conv_w[l].astype(F32), "cb": row(conv_b),
        "wa": (0.5 * _block_diag_groups(gate_a_w[l])).astype(BF16), "ba": 0.5 * row(gate_a_b),
        "wx": (0.5 * _block_diag_groups(gate_x_w[l])).astype(BF16), "bx": 0.5 * row(gate_x_b),
        "lam": row(lru_lambda),
        "nao": row(attn_out_norm), "nlo": row(lru_out_norm), "wout": bf(w_out),
        "n2": row(ffn2_norm),
        "nf": row(final_norm),
    }

    meta = meta_tokens.astype(x.dtype)[None]
    rope_meta = _rope_table(0, N_META)
    _, _, k_meta, v_meta, u_meta, _ = _front(meta, rope_meta, w, N_META)[:6]
    bsz = x.shape[0]
    rep = lambda a: jnp.broadcast_to(a, (bsz,) + a.shape[1:])
    zero_h = jnp.zeros((bsz, LRU_WIDTH), F32)
    zero_tail = jnp.zeros(((CONV_W - 1) * bsz, LRU_WIDTH), F32)
    _, h_meta, tail_meta = _lru(rep(u_meta), zero_h, zero_tail, w, N_META, True)

    rope_main = _rope_table(N_META, SEQ)
    ffn2_f32 = (ffn2_w_gate[l], ffn2_w_up[l], ffn2_w_down[l])
    h1, q, k, v, u, g, w["wg2"], w["wu2"], w["wd2"] = _front(x, rope_main, w, TM, ffn2_f32)
    y_mla, h_lru = _attention_lru(q, k, v, k_meta, v_meta, u, h_meta, tail_meta, w)
    return _back(h1, y_mla, h_lru, g, w, TM)
```

```python
import functools
import math

import jax
import jax.numpy as jnp
import numpy as np
from jax import lax
from jax.experimental import pallas as pl
from jax.experimental.pallas import tpu as pltpu

D_MODEL = 1024
SEQ = 2048
N_META = 16
CHUNK = 64
HEADS = 4
D_NOPE = 128
D_ROPE = 64
D_QK = D_NOPE + D_ROPE
D_V = 128
KV_RANK = 256
Q_RANK = 384
ROPE_THETA = 10000.0
LRU_WIDTH = 512
LRU_BLOCKS = 8
LRU_BLOCK = 64
CONV_W = 4
C_RGLRU = 8.0
D_FF = 2816
EPS = 1e-6
NEG_INF = -1e30

LANES = 128
BF16_SUBLANES = 16
D_HEAD_PAD = 2 * LANES
Z_WIDTH = Q_RANK + KV_RANK + D_ROPE + 2 * LRU_WIDTH
GATE_GROUP = 256

TM = 512
FF_CHUNK = 256
DOWN_ROWS = 256
TQ = 256
ATTN_HEADS_PER_STEP = 2
LRU_TB = 16
VMEM_LIMIT_BIG = 56 * 1024 * 1024
VMEM_LIMIT_SMALL = 40 * 1024 * 1024

F32 = jnp.float32
BF16 = jnp.bfloat16


def _rms(x, g):
    ms = jnp.mean(x * x, axis=-1, keepdims=True)
    return x * lax.rsqrt(ms + EPS) * g


def _dot(a, b):
    return jnp.dot(a, b, preferred_element_type=F32)


def _dot_nt(a, b):
    return lax.dot_general(a, b, (((1,), (1,)), ((), ())), preferred_element_type=F32)


def _gelu_tanh(x):
    c0 = math.sqrt(2.0 / math.pi)
    return x * (0.5 * jnp.tanh(x * (c0 + (c0 * 0.044715) * (x * x))) + 0.5)


def _swiglu(u, wg_ref, wu_ref, wd_ref, a_ref, hooks, finish):
    for c in range(D_FF // FF_CHUNK):
        sl = slice(c * FF_CHUNK, (c + 1) * FF_CHUNK)
        gate = _dot(u, wg_ref[:, sl])
        up = _dot(u, wu_ref[:, sl])
        a_ref[:, sl] = (jax.nn.silu(gate) * up).astype(BF16)
        if c in hooks:
            hooks[c]()
    tm = a_ref.shape[0]
    blk = min(tm, DOWN_ROWS)
    for r in range(tm // blk):
        rows = slice(r * blk, (r + 1) * blk)
        finish(rows, _dot(a_ref[rows, :], wd_ref[...]))


def _front_kernel(n_cast, x_ref, rope_ref, n1_ref, wg_ref, wu_ref, wd_ref, nmix_ref, win_ref,
                  nq_ref, wuq_ref, nkv_ref, wuk_ref, wuv_ref, qhn_ref, khn_ref, *rest):
    cast_in, rest = rest[:n_cast], rest[n_cast:]
    h_ref, q_ref, k_ref, v_ref, u_ref, g_ref = rest[:6]
    cast_out, (a_scr, un_scr) = rest[6:6 + n_cast], rest[6 + n_cast:]
    step = pl.program_id(0)
    last = pl.num_programs(0) - 1

    @pl.when(step == 0)
    def _():
        un_scr[...] = jnp.zeros_like(un_scr)

    st = {}

    def mix_in():
        z = _dot(un_scr[...], win_ref[...])
        o1 = Q_RANK
        o2 = o1 + KV_RANK
        o3 = o2 + D_ROPE
        o4 = o3 + LRU_WIDTH
        lane = lax.broadcasted_iota(jnp.int32, (z.shape[0], LANES), 1)
        st["c_q"], st["c_kv"] = z[:, :o1], z[:, o1:o2]
        st["k_r"] = jnp.where(lane < D_ROPE, z[:, o2:o2 + LANES], 0.0)
        u_ref[0] = z[:, o3:o4].astype(u_ref.dtype)
        g_ref[0] = _gelu_tanh(z[:, o4:]).astype(g_ref.dtype)

    def mix_proj():
        st["q_all"] = _dot(_rms(st["c_q"], nq_ref[...]).astype(BF16), wuq_ref[...])
        ckv = _rms(st["c_kv"], nkv_ref[...]).astype(BF16)
        st["k_nope"] = _dot(ckv, wuk_ref[...])
        v_ref[0] = _dot(ckv, wuv_ref[...]).astype(v_ref.dtype)

    def mix_heads():
        cos, sin_lo, sin_hi = rope_ref[0], rope_ref[1], rope_ref[2]

        def rope(x):
            return (x * cos + pltpu.roll(x, LANES - D_ROPE // 2, 1) * sin_lo
                    + pltpu.roll(x, D_ROPE // 2, 1) * sin_hi)

        qhn = qhn_ref[...]
        khn = khn_ref[...]
        k_r = st["k_r"]
        kr_roped = rope(k_r * khn[:, LANES:])
        kr_ss = jnp.sum(k_r * k_r, axis=-1, keepdims=True)
        scale = math.log2(math.e) / math.sqrt(D_QK)
        for hd in range(HEADS):
            qh = st["q_all"][:, hd * D_HEAD_PAD:(hd + 1) * D_HEAD_PAD]
            q_rinv = lax.rsqrt(jnp.sum(qh * qh, axis=-1, keepdims=True) / D_QK + EPS)
            qn = qh * q_rinv * qhn
            q_out = jnp.concatenate([qn[:, :LANES], rope(qn[:, LANES:])], axis=1) * scale
            q_ref[0, hd] = q_out.astype(q_ref.dtype)
            kn = st["k_nope"][:, hd * D_NOPE:(hd + 1) * D_NOPE]
            k_rinv = lax.rsqrt((jnp.sum(kn * kn, axis=-1, keepdims=True) + kr_ss) / D_QK + EPS)
            k_out = jnp.concatenate([kn * k_rinv * khn[:, :LANES], kr_roped * k_rinv], axis=1)
            k_ref[0, hd] = k_out.astype(k_ref.dtype)

    def finish(rows, ffn):
        h = x_ref[0, rows, :] + 0.5 * ffn
        h_ref[0, rows, :] = h
        un_scr[rows, :] = _rms(h, nmix_ref[...]).astype(BF16)

    def cast_weights():
        for src, dst in zip(cast_in, cast_out):
            dst[...] = src[...].astype(dst.dtype)

    @pl.when(step < last)
    def _():
        mix_in()
        u = _rms(x_ref[0], n1_ref[...]).astype(BF16)
        _swiglu(u, wg_ref, wu_ref, wd_ref, a_scr, {3: mix_proj, 6: mix_heads, 8: cast_weights}, finish)

    @pl.when(step == last)
    def _():
        mix_in()
        mix_proj()
        mix_heads()


def _resident(shape):
    nd = len(shape)
    return pl.BlockSpec(shape, lambda *_: (0,) * nd, pipeline_mode=pl.Buffered(1))


def _cast_blocks(arr, nt):
    rows = arr.shape[0]
    steps = max(d for d in range(1, nt + 1) if rows % d == 0 and (rows // d) % BF16_SUBLANES == 0)
    return pl.BlockSpec((rows // steps, arr.shape[1]), lambda n: (jnp.minimum(n, steps - 1), 0))


def _front(x, rope_tab, w, tm, cast=()):
    bsz, s, _ = x.shape
    ns = s // tm
    nt = bsz * ns
    cast_in_specs = [_cast_blocks(a, nt) for a in cast]
    cast_out_specs = [_cast_blocks(a, nt) for a in cast]
    cur = lambda n: jnp.minimum(n, nt - 1)
    prv = lambda n: jnp.maximum(n - 1, 0)
    tok_cur = lambda n: (cur(n) // ns, cur(n) % ns, 0)
    tok_prv = lambda n: (prv(n) // ns, prv(n) % ns, 0)
    head_prv = lambda n: (prv(n) // ns, 0, prv(n) % ns, 0)
    in_specs = [
        pl.BlockSpec((1, tm, D_MODEL), tok_cur),
        pl.BlockSpec((3, tm, LANES), lambda n: (0, prv(n) % ns, 0)),
        _resident((1, D_MODEL)),
        _resident((D_MODEL, D_FF)), _resident((D_MODEL, D_FF)), _resident((D_FF, D_MODEL)),
        _resident((1, D_MODEL)), _resident((D_MODEL, Z_WIDTH)),
        _resident((1, Q_RANK)), _resident((Q_RANK, HEADS * D_HEAD_PAD)),
        _resident((1, KV_RANK)), _resident((KV_RANK, HEADS * D_NOPE)), _resident((KV_RANK, HEADS * D_V)),
        _resident((1, D_HEAD_PAD)), _resident((1, D_HEAD_PAD)),
    ] + cast_in_specs
    out_shape = [
        jax.ShapeDtypeStruct((bsz, s, D_MODEL), F32),
        jax.ShapeDtypeStruct((bsz, HEADS, s, D_HEAD_PAD), BF16),
        jax.ShapeDtypeStruct((bsz, HEADS, s, D_HEAD_PAD), BF16),
        jax.ShapeDtypeStruct((bsz, s, HEADS * D_V), BF16),
        jax.ShapeDtypeStruct((bsz, s, LRU_WIDTH), BF16),
        jax.ShapeDtypeStruct((bsz, s, LRU_WIDTH), BF16),
    ] + [jax.ShapeDtypeStruct(a.shape, BF16) for a in cast]
    out_specs = [
        pl.BlockSpec((1, tm, D_MODEL), tok_cur),
        pl.BlockSpec((1, HEADS, tm, D_HEAD_PAD), head_prv),
        pl.BlockSpec((1, HEADS, tm, D_HEAD_PAD), head_prv),
        pl.BlockSpec((1, tm, HEADS * D_V), tok_prv),
        pl.BlockSpec((1, tm, LRU_WIDTH), tok_prv),
        pl.BlockSpec((1, tm, LRU_WIDTH), tok_prv),
    ] + cast_out_specs
    return pl.pallas_call(
        functools.partial(_front_kernel, len(cast)),
        grid=(nt + 1,), in_specs=in_specs, out_specs=out_specs, out_shape=out_shape,
        scratch_shapes=[pltpu.VMEM((tm, D_FF), BF16), pltpu.VMEM((tm, D_MODEL), BF16)],
        compiler_params=pltpu.CompilerParams(
            dimension_semantics=("arbitrary",), vmem_limit_bytes=VMEM_LIMIT_BIG),
        name="front",
    )(x, rope_tab, w["n1"], w["wg1"], w["wu1"], w["wd1"], w["nmix"], w["win"],
      w["nq"], w["wuq"], w["nkv"], w["wuk"], w["wuv"], w["qhn"], w["khn"], *cast)


def _attn_body(q_ref, k_ref, v_ref, km_ref, vm_ref, o_ref, kx_scr, vx_scr, hooks):
    hp, seq = q_ref.shape[1], q_ref.shape[2]
    n_tiles = seq // TQ + 1
    end = N_META + seq
    for hd in range(hp):
        vcol = slice(hd * D_V, (hd + 1) * D_V)
        kx_scr[hd, :N_META, :] = km_ref[0, hd]
        kx_scr[hd, N_META:end, :] = k_ref[0, hd]
        vx_scr[hd, :N_META, :D_V] = vm_ref[0, :, vcol]
        vx_scr[hd, N_META:end, :D_V] = v_ref[0, :, vcol]
        vx_scr[hd, end:, :D_V] = jnp.zeros((n_tiles * TQ - end, D_V), BF16)
        kx_scr[hd, end:, :] = jnp.zeros((n_tiles * TQ - end, D_HEAD_PAD), BF16)
        vx_scr[hd, :, D_V:] = jnp.ones((n_tiles * TQ, D_V), BF16)

    def rows_of(t):
        return max(0, TQ * t - CHUNK), min(seq, TQ * (t + 1) - CHUNK)

    def staircase(t):
        lo, hi = rows_of(t)
        qchunk = (lo + lax.broadcasted_iota(jnp.int32, (hi - lo, TQ), 0)) // CHUNK
        kpos = TQ * t + lax.broadcasted_iota(jnp.int32, (hi - lo, TQ), 1)
        return kpos < N_META + CHUNK * (qchunk + 1)

    def scores(t, hd):
        lo, hi = rows_of(t)
        q = q_ref[0, hd, lo:hi, :]
        s_last = jnp.where(staircase(t), _dot_nt(q, kx_scr[hd, TQ * t:TQ * (t + 1), :]), NEG_INF)
        s_past = _dot_nt(q, kx_scr[hd, :TQ * t, :]) if t > 0 else None
        return s_last, s_past

    def finish(t, hd, s_last, s_past):
        lo, hi = rows_of(t)
        m = jnp.max(s_last, axis=-1, keepdims=True)
        if s_past is not None:
            m = jnp.maximum(m, jnp.max(s_past, axis=-1, keepdims=True))
        acc = _dot(jnp.exp2(s_last - m).astype(BF16), vx_scr[hd, TQ * t:TQ * (t + 1), :])
        if s_past is not None:
            acc = acc + _dot(jnp.exp2(s_past - m).astype(BF16), vx_scr[hd, :TQ * t, :])
        o_ref[0, lo:hi, hd * D_V:(hd + 1) * D_V] = (acc[:, :D_V] / acc[:, D_V:]).astype(o_ref.dtype)

    order = [t for pair in zip(range(n_tiles - 1, -1, -1), range(n_tiles)) for t in pair][:n_tiles]
    items = [(t, hd) for t in order for hd in range(hp)]
    s_next = scores(*items[0])
    for n, item in enumerate(items):
        s_cur = s_next
        s_next = scores(*items[n + 1]) if n + 1 < len(items) else None
        finish(*item, *s_cur)
        for hook in hooks.get(n, ()):
            hook()


def _lru_blocks(first, step0, u_ref, perm_ref, cw_ref, cb_ref, wa_ref, ba_ref, wx_ref, bx_ref,
                lam_ref, y_ref, ubuf, hstate):
    bsz, tt, _ = u_ref.shape
    rows = LRU_TB * bsz
    tail = (CONV_W - 1) * bsz

    def gate_tanh(xcb, w_ref, b_ref):
        parts = [_dot(xcb[:, m * GATE_GROUP:(m + 1) * GATE_GROUP], w_ref[m])
                 for m in range(LRU_WIDTH // GATE_GROUP)]
        return jnp.tanh(jnp.concatenate(parts, axis=1) + b_ref[...])

    def phases(k):
        st = {}
        t0 = k * LRU_TB

        def conv():
            perm = perm_ref[...]
            uk = jnp.concatenate([u_ref[b, t0:t0 + LRU_TB, :] for b in range(bsz)], axis=0)
            ubuf[tail:tail + rows] = _dot(perm, uk)
            xc = cb_ref[...]
            for kk in range(CONV_W):
                xc = xc + cw_ref[kk:kk + 1, :] * ubuf[kk * bsz:kk * bsz + rows, :]
            ubuf[0:tail] = ubuf[rows:rows + tail]
            st["xc"] = xc
            st["xcb"] = xc.astype(BF16)

        def decay_gate():
            neg_lam = -lam_ref[...]
            softplus = jnp.maximum(neg_lam, 0.0) + jnp.log1p(jnp.exp(-jnp.abs(neg_lam)))
            half = (0.5 * C_RGLRU) * softplus
            neg_log_a = half * gate_tanh(st["xcb"], wa_ref, ba_ref) + half
            a = jnp.exp2(neg_log_a * (-math.log2(math.e)))
            m2 = jnp.tanh(neg_log_a) * (a * a + 1.0)
            mult = jnp.where(m2 > 0.0, m2 * lax.rsqrt(m2), 0.0)
            if first and k == 0:
                row = lax.broadcasted_iota(jnp.int32, (rows, LRU_WIDTH), 0)
                mult = jnp.where((row < bsz) & step0, 1.0, mult)
            st["a"], st["mult"] = a, mult

        def recurrence():
            a = st["a"]
            i_gate = 0.5 * gate_tanh(st["xcb"], wx_ref, bx_ref) + 0.5
            bt = (st["mult"] * st["xc"]) * i_gate
            h = hstate[...]
            hs = []
            for t in range(LRU_TB):
                sl = slice(t * bsz, (t + 1) * bsz)
                h = a[sl] * h + bt[sl]
                hs.append(h)
            hstate[...] = h
            st["h"] = jnp.concatenate(hs, axis=0)

        def output():
            hn = _dot(perm_ref[...], st["h"].astype(BF16))
            for b in range(bsz):
                y_ref[b, t0:t0 + LRU_TB, :] = hn[b * LRU_TB:(b + 1) * LRU_TB].astype(y_ref.dtype)

        return [conv, decay_gate, recurrence, output]

    return [ph for k in range(tt // LRU_TB) for ph in phases(k)]


def _lru_kernel(first, u_ref, hin_ref, tin_ref, perm_ref, cw_ref, cb_ref, wa_ref, ba_ref,
                wx_ref, bx_ref, lam_ref, y_ref, hout_ref, tout_ref, ubuf, hstate):
    step0 = pl.program_id(0) == 0

    @pl.when(step0)
    def _():
        ubuf[0:tin_ref.shape[0]] = tin_ref[...]
        hstate[...] = hin_ref[...]

    for blk in _lru_blocks(first, step0, u_ref, perm_ref, cw_ref, cb_ref, wa_ref, ba_ref,
                           wx_ref, bx_ref, lam_ref, y_ref, ubuf, hstate):
        blk()
    hout_ref[...] = hstate[...]
    tout_ref[...] = ubuf[0:tout_ref.shape[0]]


def _attn_lru_kernel(q_ref, k_ref, v_ref, km_ref, vm_ref, u_ref, hin_ref, tin_ref, perm_ref,
                     cw_ref, cb_ref, wa_ref, ba_ref, wx_ref, bx_ref, lam_ref, o_ref, y_ref,
                     kx_scr, vx_scr, ubuf, hstate):
    step0 = (pl.program_id(0) == 0) & (pl.program_id(1) == 0)

    @pl.when(step0)
    def _():
        ubuf[0:tin_ref.shape[0]] = tin_ref[...]
        hstate[...] = hin_ref[...]

    stages = _lru_blocks(False, step0, u_ref, perm_ref, cw_ref, cb_ref, wa_ref, ba_ref,
                         wx_ref, bx_ref, lam_ref, y_ref, ubuf, hstate)
    n_items = (q_ref.shape[2] // TQ + 1) * q_ref.shape[1]
    assert len(stages) <= n_items
    hooks = {n: [stage] for n, stage in enumerate(stages)}
    _attn_body(q_ref, k_ref, v_ref, km_ref, vm_ref, o_ref, kx_scr, vx_scr, hooks)


def _lru_specs(bsz, tt, tok, const2, const3):
    rows = LRU_TB * bsz
    tail = (CONV_W - 1) * bsz
    ngrp = LRU_WIDTH // GATE_GROUP
    return [
        pl.BlockSpec((bsz, tt, LRU_WIDTH), tok),
        pl.BlockSpec((bsz, LRU_WIDTH), const2),
        pl.BlockSpec((tail, LRU_WIDTH), const2),
        pl.BlockSpec((rows, rows), const2),
        pl.BlockSpec((CONV_W, LRU_WIDTH), const2),
        pl.BlockSpec((1, LRU_WIDTH), const2),
        pl.BlockSpec((ngrp, GATE_GROUP, GATE_GROUP), const3),
        pl.BlockSpec((1, LRU_WIDTH), const2),
        pl.BlockSpec((ngrp, GATE_GROUP, GATE_GROUP), const3),
        pl.BlockSpec((1, LRU_WIDTH), const2),
        pl.BlockSpec((1, LRU_WIDTH), const2),
    ]


def _lru_operands(u, h_in, tail_in, w):
    bsz = u.shape[0]
    assert LRU_TB == bsz
    rows = LRU_TB * bsz
    eye = jnp.eye(rows, dtype=BF16).reshape(bsz, LRU_TB, rows)
    perm = eye.transpose(1, 0, 2).reshape(rows, rows)
    return (u, h_in, tail_in, perm, w["cw"], w["cb"], w["wa"], w["ba"], w["wx"], w["bx"], w["lam"])


def _lru_scratch(bsz):
    rows = LRU_TB * bsz
    tail = (CONV_W - 1) * bsz
    return [pltpu.VMEM((tail + rows, LRU_WIDTH), F32), pltpu.VMEM((bsz, LRU_WIDTH), F32)]


def _lru(u, h_in, tail_in, w, tt, first):
    bsz, s, _ = u.shape
    tail = (CONV_W - 1) * bsz
    tok = lambda i: (0, i, 0)
    const2 = lambda i: (0, 0)
    const3 = lambda i: (0, 0, 0)
    out_shape = [
        jax.ShapeDtypeStruct((bsz, s, LRU_WIDTH), BF16),
        jax.ShapeDtypeStruct((bsz, LRU_WIDTH), F32),
        jax.ShapeDtypeStruct((tail, LRU_WIDTH), F32),
    ]
    out_specs = [
        pl.BlockSpec((bsz, tt, LRU_WIDTH), tok),
        pl.BlockSpec((bsz, LRU_WIDTH), const2),
        pl.BlockSpec((tail, LRU_WIDTH), const2),
    ]
    return pl.pallas_call(
        functools.partial(_lru_kernel, first),
        grid=(s // tt,), in_specs=_lru_specs(bsz, tt, tok, const2, const3),
        out_specs=out_specs, out_shape=out_shape, scratch_shapes=_lru_scratch(bsz),
        compiler_params=pltpu.CompilerParams(
            dimension_semantics=("arbitrary",), vmem_limit_bytes=VMEM_LIMIT_SMALL),
        name="lru_meta",
    )(*_lru_operands(u, h_in, tail_in, w))


def _attention_lru(q, k, v, k_meta, v_meta, u, h_in, tail_in, w):
    bsz, _, s, _ = q.shape
    hp = ATTN_HEADS_PER_STEP
    ngroups = HEADS // hp
    tt = s // (bsz * ngroups)
    assert tt % LRU_TB == 0
    in_specs = [
        pl.BlockSpec((1, hp, s, D_HEAD_PAD), lambda b, h: (b, h, 0, 0)),
        pl.BlockSpec((1, hp, s, D_HEAD_PAD), lambda b, h: (b, h, 0, 0)),
        pl.BlockSpec((1, s, hp * D_V), lambda b, h: (b, 0, h)),
        pl.BlockSpec((1, hp, N_META, D_HEAD_PAD), lambda b, h: (0, h, 0, 0)),
        pl.BlockSpec((1, N_META, hp * D_V), lambda b, h: (0, 0, h)),
    ] + _lru_specs(bsz, tt, lambda b, h: (0, b * ngroups + h, 0), lambda b, h: (0, 0),
                   lambda b, h: (0, 0, 0))
    out_specs = [
        pl.BlockSpec((1, s, hp * D_V), lambda b, h: (b, 0, h)),
        pl.BlockSpec((bsz, tt, LRU_WIDTH), lambda b, h: (0, b * ngroups + h, 0)),
    ]
    out_shape = [
        jax.ShapeDtypeStruct((bsz, s, HEADS * D_V), BF16),
        jax.ShapeDtypeStruct((bsz, s, LRU_WIDTH), BF16),
    ]
    return pl.pallas_call(
        _attn_lru_kernel,
        grid=(bsz, ngroups), in_specs=in_specs, out_specs=out_specs, out_shape=out_shape,
        scratch_shapes=[pltpu.VMEM((hp, s + TQ, D_HEAD_PAD), BF16), pltpu.VMEM((hp, s + TQ, 2 * D_V), BF16)]
        + _lru_scratch(bsz),
        compiler_params=pltpu.CompilerParams(
            dimension_semantics=("arbitrary", "arbitrary"),
            vmem_limit_bytes=VMEM_LIMIT_SMALL),
        name="attn_lru",
    )(q, k, v, k_meta, v_meta, *_lru_operands(u, h_in, tail_in, w))


def _back_kernel(h_ref, ym_ref, hl_ref, gg_ref, nao_ref, nlo_ref, wout_ref, n2_ref, wg_ref, wu_ref, wd_ref,
                 nf_ref, o_ref, a_scr, u2_scr, h2_scr):
    step = pl.program_id(0)
    st = {}

    def mix_out():
        ym = _rms(ym_ref[0].astype(F32), nao_ref[...]).astype(BF16)
        yl = _rms(hl_ref[0].astype(F32) * gg_ref[0].astype(F32), nlo_ref[...]).astype(BF16)
        half = HEADS * D_V
        h2 = h_ref[0] + _dot(ym, wout_ref[:half, :]) + _dot(yl, wout_ref[half:, :])
        st["h2"] = h2
        st["u2"] = _rms(h2, n2_ref[...]).astype(BF16)

    def finish(rows, ffn):
        o_ref[0, rows, :] = _rms(h2_scr[rows, :] + 0.5 * ffn, nf_ref[...])

    def carry():
        u2_scr[...] = st["u2"]
        h2_scr[...] = st["h2"]

    @pl.when(step == 0)
    def _():
        mix_out()
        carry()

    @pl.when(step > 0)
    def _():
        _swiglu(u2_scr[...], wg_ref, wu_ref, wd_ref, a_scr, {4: mix_out}, finish)
        carry()


def _back(h1, y_mla, h_lru, gelu_gate, w, tm):
    bsz, s, _ = h1.shape
    ns = s // tm
    nt = bsz * ns
    cur = lambda n: jnp.minimum(n, nt - 1)
    prv = lambda n: jnp.maximum(n - 1, 0)
    tok_cur = lambda n: (cur(n) // ns, cur(n) % ns, 0)
    tok_prv = lambda n: (prv(n) // ns, prv(n) % ns, 0)
    in_specs = [
        pl.BlockSpec((1, tm, D_MODEL), tok_cur),
        pl.BlockSpec((1, tm, HEADS * D_V), tok_cur),
        pl.BlockSpec((1, tm, LRU_WIDTH), tok_cur),
        pl.BlockSpec((1, tm, LRU_WIDTH), tok_cur),
        _resident((1, HEADS * D_V)), _resident((1, LRU_WIDTH)),
        _resident((HEADS * D_V + LRU_WIDTH, D_MODEL)),
        _resident((1, D_MODEL)),
        _resident((D_MODEL, D_FF)), _resident((D_MODEL, D_FF)), _resident((D_FF, D_MODEL)),
        _resident((1, D_MODEL)),
    ]
    return pl.pallas_call(
        _back_kernel,
        grid=(nt + 1,), in_specs=in_specs,
        out_specs=pl.BlockSpec((1, tm, D_MODEL), tok_prv),
        out_shape=jax.ShapeDtypeStruct((bsz, s, D_MODEL), F32),
        scratch_shapes=[pltpu.VMEM((tm, D_FF), BF16), pltpu.VMEM((tm, D_MODEL), BF16),
                        pltpu.VMEM((tm, D_MODEL), F32)],
        compiler_params=pltpu.CompilerParams(
            dimension_semantics=("arbitrary",), vmem_limit_bytes=VMEM_LIMIT_BIG),
        name="back",
    )(h1, y_mla, h_lru, gelu_gate, w["nao"], w["nlo"], w["wout"], w["n2"], w["wg2"], w["wu2"], w["wd2"],
      w["nf"])


def _rope_table(first_pos, n):
    half = D_ROPE // 2
    inv_freq = np.float32(ROPE_THETA) ** (-np.arange(0, half, dtype=np.float32) / np.float32(half))
    ang = np.arange(first_pos, first_pos + n, dtype=np.float32)[:, None] * inv_freq[None, :]
    cos, sin = np.cos(ang), np.sin(ang)
    zeros = np.zeros_like(cos)
    pad = np.zeros((n, LANES - D_ROPE), np.float32)
    return jnp.asarray(np.stack([
        np.concatenate([cos, cos, pad], axis=1),
        np.concatenate([-sin, zeros, pad], axis=1),
        np.concatenate([zeros, sin, pad], axis=1),
    ]).astype(np.float32))


def _block_diag_groups(wgt):
    per = GATE_GROUP // LRU_BLOCK
    w4 = wgt.reshape(LRU_BLOCKS // per, per, LRU_BLOCK, LRU_BLOCK)
    eye = jnp.eye(per, dtype=wgt.dtype)
    return jnp.einsum('mnij,nk->mnikj', w4, eye).reshape(LRU_BLOCKS // per, GATE_GROUP, GATE_GROUP)


def kernel(x, meta_tokens, ffn1_norm, ffn1_w_gate, ffn1_w_up, ffn1_w_down, mix_norm, w_in, q_latent_norm, w_uq, kv_latent_norm, w_uk, w_uv, q_head_norm, k_head_norm, conv_w, conv_b, gate_a_w, gate_a_b, gate_x_w, gate_x_b, lru_lambda, attn_out_norm, lru_out_norm, w_out, ffn2_norm, ffn2_w_gate, ffn2_w_up, ffn2_w_down, final_norm):
    l = 0
    row = lambda a: a[l].reshape(1, -1).astype(F32)
    bf = lambda a: a[l].astype(BF16)

    wuq_p = jnp.pad(w_uq[l].reshape(Q_RANK, HEADS, D_QK),
                    ((0, 0), (0, 0), (0, D_HEAD_PAD - D_QK))).reshape(Q_RANK, HEADS * D_HEAD_PAD)
    pad_head = lambda a: jnp.pad(a[l].astype(F32), (0, D_HEAD_PAD - D_QK)).reshape(1, D_HEAD_PAD)
    w = {
        "n1": row(ffn1_norm), "wg1": bf(ffn1_w_gate), "wu1": bf(ffn1_w_up), "wd1": bf(ffn1_w_down),
        "nmix": row(mix_norm), "win": bf(w_in),
        "nq": row(q_latent_norm), "wuq": wuq_p.astype(BF16),
        "nkv": row(kv_latent_norm), "wuk": bf(w_uk), "wuv": bf(w_uv),
        "qhn": pad_head(q_head_norm), "khn": pad_head(k_head_norm),
        "cw": conv_w[l].astype(F32), "cb": row(conv_b),
        "wa": (0.5 * _block_diag_groups(gate_a_w[l])).astype(BF16), "ba": 0.5 * row(gate_a_b),
        "wx": (0.5 * _block_diag_groups(gate_x_w[l])).astype(BF16), "bx": 0.5 * row(gate_x_b),
        "lam": row(lru_lambda),
        "nao": row(attn_out_norm), "nlo": row(lru_out_norm), "wout": bf(w_out),
        "n2": row(ffn2_norm),
        "nf": row(final_norm),
    }

    meta = meta_tokens.astype(x.dtype)[None]
    rope_meta = _rope_table(0, N_META)
    _, _, k_meta, v_meta, u_meta, _ = _front(meta, rope_meta, w, N_META)[:6]
    bsz = x.shape[0]
    rep = lambda a: jnp.broadcast_to(a, (bsz,) + a.shape[1:])
    zero_h = jnp.zeros((bsz, LRU_WIDTH), F32)
    zero_tail = jnp.zeros(((CONV_W - 1) * bsz, LRU_WIDTH), F32)
    _, h_meta, tail_meta = _lru(rep(u_meta), zero_h, zero_tail, w, N_META, True)

    rope_main = _rope_table(N_META, SEQ)
    ffn2_f32 = (ffn2_w_gate[l], ffn2_w_up[l], ffn2_w_down[l])
    h1, q, k, v, u, g, w["wg2"], w["wu2"], w["wd2"] = _front(x, rope_main, w, TM, ffn2_f32)
    y_mla, h_lru = _attention_lru(q, k, v, k_meta, v_meta, u, h_meta, tail_meta, w)
    return _back(h1, y_mla, h_lru, g, w, TM)
```
